```python
import math, functools
import jax, jax.numpy as jnp
from jax import lax
import numpy as np

D_MODEL = 4096
BATCH = 8
SEQ = 2048
DEPTH = 4
DEC_BATCH = 1
DEC_SEQ = 16384
PAST_LEN = 128

N_MIXERS = 2
D_RNN = D_MODEL
N_RG_BLOCKS = 16
RG_BLOCK = D_RNN // N_RG_BLOCKS
CONV_WIDTH = 4
CONV_PAD_LEFT = (CONV_WIDTH - 1) // 2
CONV_PAD_RIGHT = CONV_WIDTH - 1 - CONV_PAD_LEFT
RG_C = 8.0
POOL_WINDOWS = (2, 4, 8, 16)
N_POOL_GROUPS = len(POOL_WINDOWS)
POOL_GROUP = D_MODEL // N_POOL_GROUPS
D_FF = D_MODEL
N_EXPERTS = 8
TOP_K = 2
D_FF_EXPERT = D_MODEL // 8
NORM_EPS = 1e-6
N_RG_LAYERS = (DEPTH + 1) // 2
N_POOL_LAYERS = DEPTH // 2
N_DENSE_LAYERS = (DEPTH + 1) // 2
N_MOE_LAYERS = DEPTH // 2

kernel_name = "hybrid_rglru_pool_moe_encoder"


def _rmsnorm(x, g):
    xf = x.astype(jnp.float32)
    y = xf * lax.rsqrt(jnp.mean(xf * xf, axis=-1, keepdims=True) + NORM_EPS)
    return (y * g.astype(jnp.float32)).astype(x.dtype)


def _centred_dwconv(u, w, b):
    y = lax.conv_general_dilated(
        u, w[:, None, :], window_strides=(1,), padding=[(CONV_PAD_LEFT, CONV_PAD_RIGHT)],
        dimension_numbers=("NWC", "WIO", "NWC"), feature_group_count=u.shape[-1])
    return y + b


def _block_diag(u, w, b):
    ub = u.reshape(u.shape[:-1] + (N_RG_BLOCKS, RG_BLOCK))
    return jnp.einsum("bshi,hij->bshj", ub, w).reshape(u.shape) + b


def _linear_scan(a, bx, reverse):
    def combine(left, right):
        a_l, b_l = left
        a_r, b_r = right
        return a_l * a_r, a_r * b_l + b_r
    _, h = lax.associative_scan(combine, (a, bx), axis=1, reverse=reverse)
    return h


def _rglru_direction(u, w_a, b_a, w_i, b_i, lam, reverse):
    r = jax.nn.sigmoid(_block_diag(u, w_a, b_a).astype(jnp.float32))
    i = jax.nn.sigmoid(_block_diag(u, w_i, b_i).astype(jnp.float32))
    log_a = -RG_C * r * jax.nn.softplus(-lam.astype(jnp.float32))
    a = jnp.exp(log_a)
    norm = jnp.sqrt(-jnp.expm1(2.0 * log_a))
    return _linear_scan(a, norm * i * u.astype(jnp.float32), reverse)


def _rglru_block(xn, w_in, conv_w, conv_b, w_a, b_a, w_i, b_i, lam, w_out):
    proj = xn @ w_in
    gate_branch, rec_branch = jnp.split(proj, 2, axis=-1)
    u = _centred_dwconv(rec_branch, conv_w, conv_b)
    h = (_rglru_direction(u, w_a[0], b_a[0], w_i[0], b_i[0], lam[0], False)
         + _rglru_direction(u, w_a[1], b_a[1], w_i[1], b_i[1], lam[1], True))
    y = (h * jax.nn.gelu(gate_branch.astype(jnp.float32))).astype(xn.dtype)
    return y @ w_out


def _pool_mixer(xn, w_grp, scale):
    B, S, D = xn.shape
    xf = xn.astype(jnp.float32)
    csum = jnp.concatenate([jnp.zeros((B, 1, D), jnp.float32), jnp.cumsum(xf, axis=1)], axis=1)
    t = np.arange(S)
    outs = []
    for g, win in enumerate(POOL_WINDOWS):
        lo = np.clip(t - win // 2, 0, S)
        hi = np.clip(t + win - win // 2, 0, S)
        sl = slice(g * POOL_GROUP, (g + 1) * POOL_GROUP)
        cg = csum[..., sl]
        wsum = jnp.take(cg, jnp.asarray(hi, jnp.int32), axis=1) - jnp.take(cg, jnp.asarray(lo, jnp.int32), axis=1)
        cnt = jnp.asarray(hi - lo, jnp.float32)[None, :, None]
        outs.append(wsum / cnt - xf[..., sl])
    d = jnp.stack(outs, axis=2).astype(xn.dtype)
    y = jnp.einsum("bsgi,gij->bsgj", d, w_grp).reshape(B, S, D)
    return y * scale


def _swiglu(x, w_gate, w_up, w_down):
    return (jax.nn.silu(x @ w_gate) * (x @ w_up)) @ w_down


def _moe(xn, w_router, w_gate, w_up, w_down):
    B, S, D = xn.shape
    xt = xn.reshape(-1, D)
    logits = (xt @ w_router).astype(jnp.float32)
    top_val, top_idx = lax.top_k(logits, TOP_K)
    gates = jax.nn.softmax(top_val, axis=-1)
    dense_gates = jnp.sum(jax.nn.one_hot(top_idx, N_EXPERTS, dtype=jnp.float32) * gates[..., None], axis=1)
    h = jax.nn.silu(jnp.einsum("td,edf->tef", xt, w_gate)) * jnp.einsum("td,edf->tef", xt, w_up)
    h = h * dense_gates[..., None].astype(h.dtype)
    out = jnp.einsum("tef,efd->td", h, w_down)
    return out.reshape(B, S, D)


def _trunk(x, mix_norm, ffn_norm, final_norm, rg_w_in, rg_conv_w, rg_conv_b, rg_w_a, rg_b_a,
           rg_w_i, rg_b_i, rg_lambda, rg_w_out, pool_w, pool_scale, ffn_w_gate, ffn_w_up,
           ffn_w_down, moe_router, moe_w_gate, moe_w_up, moe_w_down):
    for i in range(DEPTH):
        j = i // 2
        h = _rmsnorm(x, mix_norm[i])
        if i % N_MIXERS == 0:
            x = x + _rglru_block(h, rg_w_in[j], rg_conv_w[j], rg_conv_b[j], rg_w_a[j], rg_b_a[j],
                                 rg_w_i[j], rg_b_i[j], rg_lambda[j], rg_w_out[j])
        else:
            x = x + _pool_mixer(h, pool_w[j], pool_scale[j])
        h = _rmsnorm(x, ffn_norm[i])
        if i % 2 == 0:
            x = x + _swiglu(h, ffn_w_gate[j], ffn_w_up[j], ffn_w_down[j])
        else:
            x = x + _moe(h, moe_router[j], moe_w_gate[j], moe_w_up[j], moe_w_down[j])
    return _rmsnorm(x, final_norm)


def _uniform(key, shape, std):
    bound = std * math.sqrt(3.0)
    return jax.random.uniform(key, shape, jnp.float32, -bound, bound)


def setup_inputs(seed: int = 0) -> dict:
    key = jax.random.key(seed)
    ks = jax.random.split(key, 24)
    d_in = D_MODEL ** -0.5
    a0 = jax.random.uniform(ks[13], (N_RG_LAYERS, 2, D_RNN), jnp.float32, 0.9, 0.999)
    s = a0 ** (1.0 / RG_C)
    rg_lambda = jnp.log(s) - jnp.log1p(-s)
    return {
        "x_prompt": jax.random.normal(ks[0], (BATCH, SEQ, D_MODEL), jnp.float32),
        "x_sample": jax.random.normal(ks[1], (DEC_BATCH, DEC_SEQ, D_MODEL), jnp.float32),
        "mix_norm": 1.0 + 0.02 * jax.random.normal(ks[2], (DEPTH, D_MODEL), jnp.float32),
        "ffn_norm": 1.0 + 0.02 * jax.random.normal(ks[3], (DEPTH, D_MODEL), jnp.float32),
        "final_norm": 1.0 + 0.02 * jax.random.normal(ks[4], (D_MODEL,), jnp.float32),
        "rg_w_in": _uniform(ks[5], (N_RG_LAYERS, D_MODEL, 2 * D_RNN), d_in),
        "rg_conv_w": _uniform(ks[6], (N_RG_LAYERS, CONV_WIDTH, D_RNN), CONV_WIDTH ** -0.5),
        "rg_conv_b": 0.02 * jax.random.normal(ks[7], (N_RG_LAYERS, D_RNN), jnp.float32),
        "rg_w_a": _uniform(ks[8], (N_RG_LAYERS, 2, N_RG_BLOCKS, RG_BLOCK, RG_BLOCK), RG_BLOCK ** -0.5),
        "rg_b_a": 0.02 * jax.random.normal(ks[9], (N_RG_LAYERS, 2, D_RNN), jnp.float32),
        "rg_w_i": _uniform(ks[10], (N_RG_LAYERS, 2, N_RG_BLOCKS, RG_BLOCK, RG_BLOCK), RG_BLOCK ** -0.5),
        "rg_b_i": 0.02 * jax.random.normal(ks[11], (N_RG_LAYERS, 2, D_RNN), jnp.float32),
        "rg_lambda": rg_lambda,
        "rg_w_out": _uniform(ks[12], (N_RG_LAYERS, D_RNN, D_MODEL), D_RNN ** -0.5),
        "pool_w": _uniform(ks[14], (N_POOL_LAYERS, N_POOL_GROUPS, POOL_GROUP, POOL_GROUP), POOL_GROUP ** -0.5),
        "pool_scale": 1.0 + 0.02 * jax.random.normal(ks[15], (N_POOL_LAYERS, D_MODEL), jnp.float32),
        "ffn_w_gate": _uniform(ks[16], (N_DENSE_LAYERS, D_MODEL, D_FF), d_in),
        "ffn_w_up": _uniform(ks[17], (N_DENSE_LAYERS, D_MODEL, D_FF), d_in),
        "ffn_w_down": _uniform(ks[18], (N_DENSE_LAYERS, D_FF, D_MODEL), D_FF ** -0.5),
        "moe_router": _uniform(ks[19], (N_MOE_LAYERS, D_MODEL, N_EXPERTS), d_in),
        "moe_w_gate": _uniform(ks[20], (N_MOE_LAYERS, N_EXPERTS, D_MODEL, D_FF_EXPERT), d_in),
        "moe_w_up": _uniform(ks[21], (N_MOE_LAYERS, N_EXPERTS, D_MODEL, D_FF_EXPERT), d_in),
        "moe_w_down": _uniform(ks[22], (N_MOE_LAYERS, N_EXPERTS, D_FF_EXPERT, D_MODEL), D_FF_EXPERT ** -0.5),
    }


def reference(x_prompt, x_sample, mix_norm, ffn_norm, final_norm, rg_w_in, rg_conv_w, rg_conv_b,
              rg_w_a, rg_b_a, rg_w_i, rg_b_i, rg_lambda, rg_w_out, pool_w, pool_scale,
              ffn_w_gate, ffn_w_up, ffn_w_down, moe_router, moe_w_gate, moe_w_up, moe_w_down):
    trunk = functools.partial(
        _trunk, mix_norm=mix_norm, ffn_norm=ffn_norm, final_norm=final_norm, rg_w_in=rg_w_in,
        rg_conv_w=rg_conv_w, rg_conv_b=rg_conv_b, rg_w_a=rg_w_a, rg_b_a=rg_b_a, rg_w_i=rg_w_i,
        rg_b_i=rg_b_i, rg_lambda=rg_lambda, rg_w_out=rg_w_out, pool_w=pool_w, pool_scale=pool_scale,
        ffn_w_gate=ffn_w_gate, ffn_w_up=ffn_w_up, ffn_w_down=ffn_w_down, moe_router=moe_router,
        moe_w_gate=moe_w_gate, moe_w_up=moe_w_up, moe_w_down=moe_w_down)
    y_prompt = trunk(x_prompt)
    y_sample = trunk(x_sample)
    return (y_prompt, y_sample)
```

```python
import functools
import math

import jax
import jax.numpy as jnp
from jax import lax
from jax.experimental import pallas as pl
from jax.experimental.pallas import tpu as pltpu

NORM_EPS = 1e-6
RG_C = 8.0
RG_BLOCK = 256
CONV_WIDTH = 4
CONV_PAD_LEFT = (CONV_WIDTH - 1) // 2
POOL_WINDOWS = (2, 4, 8, 16)
TOP_K = 2
SUBLANES = 8
LANES = 128
HALO = 8
VMEM_LIMIT = 56 * 1024 * 1024

_BF16 = jnp.bfloat16
_F32 = jnp.float32


def _params(*sem):
    return pltpu.CompilerParams(dimension_semantics=sem, vmem_limit_bytes=VMEM_LIMIT)


def _store_row_rstd(rs_ref, x_ref):
    tm = x_ref.shape[0]
    rb = min(128, tm)

    def body(s, _):
        r0 = pl.multiple_of(s * rb, rb)
        xf = x_ref[pl.ds(r0, rb), :].astype(_F32)
        rs_ref[pl.ds(r0, rb), :] = lax.rsqrt(jnp.mean(xf * xf, axis=-1, keepdims=True) + NORM_EPS)
        return 0

    lax.fori_loop(0, tm // rb, body, 0)


def _sigmoid(x):
    return 1.0 / (1.0 + jnp.exp(-x))


def _proj_kernel(x_ref, w_ref, o_ref, rs_ref):
    @pl.when(pl.program_id(1) == 0)
    def _():
        _store_row_rstd(rs_ref, x_ref)

    o_ref[...] = jnp.dot(x_ref[...], w_ref[...], preferred_element_type=_F32) * rs_ref[...]


def _proj_mm(xb, w, *, tm, tn):
    T, K = xb.shape
    N = w.shape[1]
    tm, tn = min(tm, T), min(tn, N)
    return pl.pallas_call(
        _proj_kernel,
        out_shape=jax.ShapeDtypeStruct((T, N), _F32),
        grid=(T // tm, N // tn),
        in_specs=[pl.BlockSpec((tm, K), lambda i, j: (i, 0)),
                  pl.BlockSpec((K, tn), lambda i, j: (0, j))],
        out_specs=pl.BlockSpec((tm, tn), lambda i, j: (i, j)),
        scratch_shapes=[pltpu.VMEM((tm, 1), _F32)],
        compiler_params=_params("parallel", "arbitrary"),
        name="proj_mm",
    )(xb, w)


def _res_kernel(y_ref, w_ref, r_ref, o_ref, ob_ref):
    o = r_ref[...] + jnp.dot(y_ref[...], w_ref[...], preferred_element_type=_F32)
    o_ref[...] = o
    ob_ref[...] = o.astype(_BF16)


def _res_mm(y, w, res, *, tm, tn):
    T, K = y.shape
    N = w.shape[1]
    tm, tn = min(tm, T), min(tn, N)
    return pl.pallas_call(
        _res_kernel,
        out_shape=(jax.ShapeDtypeStruct((T, N), _F32), jax.ShapeDtypeStruct((T, N), _BF16)),
        grid=(T // tm, N // tn),
        in_specs=[pl.BlockSpec((tm, K), lambda i, j: (i, 0)),
                  pl.BlockSpec((K, tn), lambda i, j: (0, j)),
                  pl.BlockSpec((tm, tn), lambda i, j: (i, j))],
        out_specs=(pl.BlockSpec((tm, tn), lambda i, j: (i, j)),
                   pl.BlockSpec((tm, tn), lambda i, j: (i, j))),
        compiler_params=_params("parallel", "parallel"),
        name="res_mm",
    )(y, w, res)


def _glu_kernel(x_ref, wg_ref, wu_ref, o_ref, rs_ref):
    @pl.when(pl.program_id(1) == 0)
    def _():
        _store_row_rstd(rs_ref, x_ref)

    rs = rs_ref[...]
    g = jnp.dot(x_ref[...], wg_ref[...], preferred_element_type=_F32) * rs
    u = jnp.dot(x_ref[...], wu_ref[...], preferred_element_type=_F32) * rs
    o_ref[...] = (g * _sigmoid(g) * u).astype(o_ref.dtype)


def _glu_mm(xb, wg, wu, *, tm, tn):
    T, K = xb.shape
    N = wg.shape[1]
    tm, tn = min(tm, T), min(tn, N)
    return pl.pallas_call(
        _glu_kernel,
        out_shape=jax.ShapeDtypeStruct((T, N), _BF16),
        grid=(T // tm, N // tn),
        in_specs=[pl.BlockSpec((tm, K), lambda i, j: (i, 0)),
                  pl.BlockSpec((K, tn), lambda i, j: (0, j)),
                  pl.BlockSpec((K, tn), lambda i, j: (0, j))],
        out_specs=pl.BlockSpec((tm, tn), lambda i, j: (i, j)),
        scratch_shapes=[pltpu.VMEM((tm, 1), _F32)],
        compiler_params=_params("parallel", "arbitrary"),
        name="glu_mm",
    )(xb, wg, wu)


def _moe_glu_kernel(x_ref, wg_ref, wu_ref, gt_ref, o_ref, rs_ref):
    e = pl.program_id(1)

    @pl.when(e == 0)
    def _():
        _store_row_rstd(rs_ref, x_ref)

    rs = rs_ref[...]
    g = jnp.dot(x_ref[...], wg_ref[...], preferred_element_type=_F32) * rs
    u = jnp.dot(x_ref[...], wu_ref[...], preferred_element_type=_F32) * rs
    gates = gt_ref[...]
    lane = lax.broadcasted_iota(jnp.int32, gates.shape, 1)
    ge = jnp.sum(jnp.where(lane == e, gates, 0.0), axis=-1, keepdims=True)
    o_ref[...] = (g * _sigmoid(g) * u * ge).astype(o_ref.dtype)


def _moe_glu_mm(xb, wg, wu, gates, *, tm):
    T, K = xb.shape
    E, _, F = wg.shape
    tm = min(tm, T)
    return pl.pallas_call(
        _moe_glu_kernel,
        out_shape=jax.ShapeDtypeStruct((T, E * F), _BF16),
        grid=(T // tm, E),
        in_specs=[pl.BlockSpec((tm, K), lambda i, e: (i, 0)),
                  pl.BlockSpec((None, K, F), lambda i, e: (e, 0, 0)),
                  pl.BlockSpec((None, K, F), lambda i, e: (e, 0, 0)),
                  pl.BlockSpec((tm, LANES), lambda i, e: (i, 0))],
        out_specs=pl.BlockSpec((tm, F), lambda i, e: (i, e)),
        scratch_shapes=[pltpu.VMEM((tm, 1), _F32)],
        compiler_params=_params("parallel", "arbitrary"),
        name="moe_glu_mm",
    )(xb, wg, wu, gates)


def _tile_scan(a, b, row, reverse):
    for d in (1, 2, 4):
        if reverse:
            shift, valid = SUBLANES - d, row < SUBLANES - d
        else:
            shift, valid = d, row >= d
        a_s = pltpu.roll(a, shift, 0)
        b_s = pltpu.roll(b, shift, 0)
        b = b + a * jnp.where(valid, b_s, 0.0)
        a = a * jnp.where(valid, a_s, 1.0)
    return a, b


def _rg_kernel(rec_ref, prev_ref, next_ref, gate_ref, cw_ref, cb_ref, wa_ref, ba_ref, wi_ref, bi_ref,
               cn_ref, y_ref, hs_ref, ext_ref, a_ref, b_ref, carry_ref, *, ts, n_chunks, sb):
    p = pl.program_id(2)
    k = pl.program_id(3)
    t = jnp.where(p == 0, k, n_chunks - 1 - k)
    C = rec_ref.shape[-1]

    @pl.when(k == 0)
    def _():
        carry_ref[...] = jnp.zeros_like(carry_ref)

    ext_ref[pl.ds(HALO, ts), :] = rec_ref[...]
    ext_ref[pl.ds(0, HALO), :] = jnp.where(t > 0, prev_ref[...], 0.0)
    ext_ref[pl.ds(HALO + ts, HALO), :] = jnp.where(t < n_chunks - 1, next_ref[...], 0.0)

    cw = cw_ref[...]
    cb = cb_ref[...]
    ba = ba_ref[...]
    bi = bi_ref[...]
    cn = cn_ref[...]

    def gates_body(s, _):
        r0 = pl.multiple_of(s * sb, sb)
        e = ext_ref[pl.ds(r0, sb + 2 * HALO), :]
        u = cb
        for tap in range(CONV_WIDTH):
            off = HALO + tap - CONV_PAD_LEFT
            u = u + cw[tap:tap + 1, :] * e[off:off + sb, :]
        ub = u.astype(_BF16)
        ga = jnp.dot(ub, wa_ref[...], preferred_element_type=_F32) + ba
        gi = jnp.dot(ub, wi_ref[...], preferred_element_type=_F32) + bi
        a = jnp.exp(cn * _sigmoid(ga))
        a_ref[pl.ds(r0, sb), :] = a
        b_ref[pl.ds(r0, sb), :] = jnp.sqrt(1.0 - a * a) * _sigmoid(gi) * u
        return 0

    lax.fori_loop(0, ts // sb, gates_body, 0)

    row = lax.broadcasted_iota(jnp.int32, (SUBLANES, C), 0)
    n_tiles = ts // SUBLANES

    @pl.when(p == 0)
    def _():
        base = pl.multiple_of(t * ts, ts)

        def body(g, carry):
            r0 = pl.multiple_of(g * SUBLANES, SUBLANES)
            a, b = _tile_scan(a_ref[pl.ds(r0, SUBLANES), :], b_ref[pl.ds(r0, SUBLANES), :], row, False)
            h = b + a * carry
            hs_ref[pl.ds(base + r0, SUBLANES), :] = h
            return jnp.broadcast_to(h[SUBLANES - 1:SUBLANES, :], (SUBLANES, C))

        carry_ref[...] = lax.fori_loop(0, n_tiles, body, carry_ref[...], unroll=4)

    @pl.when(p == 1)
    def _():
        base = pl.multiple_of(t * ts, ts)

        def body(g, carry):
            r0 = pl.multiple_of((n_tiles - 1 - g) * SUBLANES, SUBLANES)
            a, b = _tile_scan(a_ref[pl.ds(r0, SUBLANES), :], b_ref[pl.ds(r0, SUBLANES), :], row, True)
            h = b + a * carry
            b_ref[pl.ds(r0, SUBLANES), :] = h + hs_ref[pl.ds(base + r0, SUBLANES), :]
            return jnp.broadcast_to(h[0:1, :], (SUBLANES, C))

        carry_ref[...] = lax.fori_loop(0, n_tiles, body, carry_ref[...], unroll=4)

        def out_body(s, _):
            r0 = pl.multiple_of(s * sb, sb)
            gate = gate_ref[pl.ds(r0, sb), :]
            y_ref[pl.ds(r0, sb), :] = (b_ref[pl.ds(r0, sb), :] * jax.nn.gelu(gate)).astype(y_ref.dtype)
            return 0

        lax.fori_loop(0, ts // sb, out_body, 0)


def _rg_core(proj, conv_w, conv_b, w_a, b_a, w_i, b_i, cneg, *, ts):
    B, S, D2 = proj.shape
    D = D2 // 2
    C = RG_BLOCK
    nc = D // C
    ts = min(ts, S)
    n = S // ts
    sb = min(256, ts)
    tpb = ts // HALO

    def t_of(p, k):
        return jnp.where(p == 0, k, n - 1 - k)

    def t_out(p, k):
        return jnp.where(p == 0, n - 1, n - 1 - k)

    kern = functools.partial(_rg_kernel, ts=ts, n_chunks=n, sb=sb)
    return pl.pallas_call(
        kern,
        out_shape=jax.ShapeDtypeStruct((B, S, D), _BF16),
        grid=(B, nc, 2, n),
        in_specs=[
            pl.BlockSpec((None, ts, C), lambda b, c, p, k: (b, t_of(p, k), nc + c)),
            pl.BlockSpec((None, HALO, C), lambda b, c, p, k: (b, jnp.maximum(t_of(p, k) * tpb - 1, 0), nc + c)),
            pl.BlockSpec((None, HALO, C),
                         lambda b, c, p, k: (b, jnp.minimum((t_of(p, k) + 1) * tpb, S // HALO - 1), nc + c)),
            pl.BlockSpec((None, ts, C), lambda b, c, p, k: (b, t_out(p, k), c)),
            pl.BlockSpec((CONV_WIDTH, C), lambda b, c, p, k: (0, c)),
            pl.BlockSpec((1, C), lambda b, c, p, k: (0, c)),
            pl.BlockSpec((None, None, C, C), lambda b, c, p, k: (p, c, 0, 0)),
            pl.BlockSpec((None, 1, C), lambda b, c, p, k: (p, 0, c)),
            pl.BlockSpec((None, None, C, C), lambda b, c, p, k: (p, c, 0, 0)),
            pl.BlockSpec((None, 1, C), lambda b, c, p, k: (p, 0, c)),
            pl.BlockSpec((None, 1, C), lambda b, c, p, k: (p, 0, c)),
        ],
        out_specs=pl.BlockSpec((None, ts, C), lambda b, c, p, k: (b, t_out(p, k), c)),
        scratch_shapes=[
            pltpu.VMEM((S, C), _F32),
            pltpu.VMEM((ts + 2 * HALO, C), _F32),
            pltpu.VMEM((ts, C), _F32),
            pltpu.VMEM((ts, C), _F32),
            pltpu.VMEM((SUBLANES, C), _F32),
        ],
        compiler_params=_params("parallel", "parallel", "arbitrary", "arbitrary"),
        name="rg_core",
    )(proj, proj, proj, proj, conv_w, conv_b, w_a, b_a, w_i, b_i, cneg)


def _pool_kernel(x_ref, prev_ref, next_ref, g_ref, pw_ref, ps_ref, rh_ref, rl_ref,
                 o_ref, ob_ref, gt_ref, ext_ref, *, ts, n_chunks, seq, n_experts):
    t = pl.program_id(1)
    D = x_ref.shape[-1]
    G = len(POOL_WINDOWS)
    gs = D // G
    L = ts + 2 * HALO

    def normed(v):
        return v * lax.rsqrt(jnp.mean(v * v, axis=-1, keepdims=True) + NORM_EPS) * g_ref[...]

    ext_ref[pl.ds(HALO, ts), :] = normed(x_ref[...])
    ext_ref[pl.ds(0, HALO), :] = jnp.where(t > 0, normed(prev_ref[...]), 0.0)
    ext_ref[pl.ds(HALO + ts, HALO), :] = jnp.where(t < n_chunks - 1, normed(next_ref[...]), 0.0)

    tpos = t * ts + lax.broadcasted_iota(jnp.int32, (ts, 1), 0)
    ssq = jnp.zeros((ts, 1), _F32)
    logits = jnp.zeros((ts, LANES), _F32)
    for gi, win in enumerate(POOL_WINDOWS):
        cols = pl.ds(gi * gs, gs)
        e = ext_ref[:, cols]
        s = pltpu.roll(e, 1, 0) + e
        w = 2
        while w < win:
            s = pltpu.roll(s, w // 2, 0) + pltpu.roll(s, L - w // 2, 0)
            w *= 2
        half = win // 2
        cnt = jnp.minimum(tpos + (win - half), seq) - jnp.maximum(tpos - half, 0)
        d = s[HALO:HALO + ts, :] * (1.0 / cnt.astype(_F32)) - e[HALO:HALO + ts, :]
        y = jnp.dot(d.astype(_BF16), pw_ref[gi], preferred_element_type=_F32)
        o = x_ref[:, cols] + y * ps_ref[:, cols]
        o_ref[:, cols] = o
        oh = o.astype(_BF16)
        ob_ref[:, cols] = oh
        ssq = ssq + jnp.sum(o * o, axis=-1, keepdims=True)
        ol = (o - oh.astype(_F32)).astype(_BF16)
        rh = rh_ref[cols, :]
        logits = (logits + jnp.dot(oh, rh, preferred_element_type=_F32)
                  + jnp.dot(oh, rl_ref[cols, :], preferred_element_type=_F32)
                  + jnp.dot(ol, rh, preferred_element_type=_F32))

    logits = logits * lax.rsqrt(ssq * (1.0 / D) + NORM_EPS)
    lane = lax.broadcasted_iota(jnp.int32, (ts, LANES), 1).astype(_F32)
    neg = jnp.float32(-jnp.inf)
    logits = jnp.where(lane < n_experts, logits, neg)
    m1 = jnp.max(logits, axis=-1, keepdims=True)
    i1 = jnp.min(jnp.where(logits == m1, lane, float(LANES)), axis=-1, keepdims=True)
    rest = jnp.where(lane == i1, neg, logits)
    m2 = jnp.max(rest, axis=-1, keepdims=True)
    i2 = jnp.min(jnp.where(rest == m2, lane, float(LANES)), axis=-1, keepdims=True)
    e2 = jnp.exp(m2 - m1)
    g1 = 1.0 / (1.0 + e2)
    gt_ref[...] = jnp.where(lane == i1, g1, 0.0) + jnp.where(lane == i2, e2 * g1, 0.0)


def _pool_mix(x, gain, pool_w, pool_scale, r_hi, r_lo, n_experts, *, ts):
    B, S, D = x.shape
    ts = min(ts, S)
    n = S // ts
    tpb = ts // HALO
    G, gs, _ = pool_w.shape
    kern = functools.partial(_pool_kernel, ts=ts, n_chunks=n, seq=S, n_experts=n_experts)
    o, ob, gt = pl.pallas_call(
        kern,
        out_shape=(jax.ShapeDtypeStruct((B, S, D), _F32), jax.ShapeDtypeStruct((B, S, D), _BF16),
                   jax.ShapeDtypeStruct((B, S, LANES), _F32)),
        grid=(B, n),
        in_specs=[
            pl.BlockSpec((None, ts, D), lambda b, t: (b, t, 0)),
            pl.BlockSpec((None, HALO, D), lambda b, t: (b, jnp.maximum(t * tpb - 1, 0), 0)),
            pl.BlockSpec((None, HALO, D), lambda b, t: (b, jnp.minimum((t + 1) * tpb, S // HALO - 1), 0)),
            pl.BlockSpec((1, D), lambda b, t: (0, 0)),
            pl.BlockSpec((G, gs, gs), lambda b, t: (0, 0, 0)),
            pl.BlockSpec((1, D), lambda b, t: (0, 0)),
            pl.BlockSpec((D, LANES), lambda b, t: (0, 0)),
            pl.BlockSpec((D, LANES), lambda b, t: (0, 0)),
        ],
        out_specs=(pl.BlockSpec((None, ts, D), lambda b, t: (b, t, 0)),
                   pl.BlockSpec((None, ts, D), lambda b, t: (b, t, 0)),
                   pl.BlockSpec((None, ts, LANES), lambda b, t: (b, t, 0))),
        scratch_shapes=[pltpu.VMEM((ts + 2 * HALO, D), _F32)],
        compiler_params=_params("parallel", "parallel"),
        name="pool_mix",
    )(x, x, x, gain, pool_w, pool_scale, r_hi, r_lo)
    return o, ob, gt


def _norm_kernel(x_ref, g_ref, o_ref):
    x = x_ref[...]
    o_ref[...] = x * lax.rsqrt(jnp.mean(x * x, axis=-1, keepdims=True) + NORM_EPS) * g_ref[...]


def _final_norm(x, gain, *, tm):
    T, D = x.shape
    tm = min(tm, T)
    return pl.pallas_call(
        _norm_kernel,
        out_shape=jax.ShapeDtypeStruct((T, D), _F32),
        grid=(T // tm,),
        in_specs=[pl.BlockSpec((tm, D), lambda i: (i, 0)), pl.BlockSpec((1, D), lambda i: (0, 0))],
        out_specs=pl.BlockSpec((tm, D), lambda i: (i, 0)),
        compiler_params=_params("parallel"),
        name="final_norm",
    )(x, gain)


def _fold(gain, w):
    return (gain[:, None] * w).astype(_BF16)


def _prep_weights(mix_norm, ffn_norm, final_norm, rg_w_in, rg_conv_w, rg_conv_b, rg_w_a, rg_b_a, rg_w_i,
                  rg_b_i, rg_lambda, rg_w_out, pool_w, pool_scale, ffn_w_gate, ffn_w_up, ffn_w_down,
                  moe_router, moe_w_gate, moe_w_up, moe_w_down):
    depth, D = mix_norm.shape
    layers = []
    for i in range(depth):
        j = i // 2
        lw = {}
        if i % 2 == 0:
            lw["w_in"] = _fold(mix_norm[i], rg_w_in[j])
            lw["conv_w"] = rg_conv_w[j]
            lw["conv_b"] = rg_conv_b[j][None, :]
            lw["w_a"] = rg_w_a[j].astype(_BF16)
            lw["b_a"] = rg_b_a[j][:, None, :]
            lw["w_i"] = rg_w_i[j].astype(_BF16)
            lw["b_i"] = rg_b_i[j][:, None, :]
            lw["cneg"] = (-RG_C * jax.nn.softplus(-rg_lambda[j]))[:, None, :]
            lw["w_out"] = rg_w_out[j].astype(_BF16)
            lw["w_gate"] = _fold(ffn_norm[i], ffn_w_gate[j])
            lw["w_up"] = _fold(ffn_norm[i], ffn_w_up[j])
            lw["w_down"] = ffn_w_down[j].astype(_BF16)
        else:
            E = moe_router.shape[-1]
            lw["mix_gain"] = mix_norm[i][None, :]
            lw["pool_w"] = pool_w[j].astype(_BF16)
            lw["pool_scale"] = pool_scale[j][None, :]
            router = jnp.pad(ffn_norm[i][:, None] * moe_router[j], ((0, 0), (0, LANES - E)))
            lw["r_hi"] = router.astype(_BF16)
            lw["r_lo"] = (router - lw["r_hi"].astype(_F32)).astype(_BF16)
            lw["n_experts"] = E
            lw["w_gate"] = _fold(ffn_norm[i], moe_w_gate[j])
            lw["w_up"] = _fold(ffn_norm[i], moe_w_up[j])
            lw["w_down"] = moe_w_down[j].reshape(-1, D).astype(_BF16)
        layers.append(lw)
    return layers, final_norm[None, :]


def _trunk(x, layers, final_gain, *, tm, tn, rg_ts, pool_ts):
    B, S, D = x.shape
    T = B * S
    xf = x.reshape(T, D)
    xb = xf.astype(_BF16)
    for i, lw in enumerate(layers):
        if i % 2 == 0:
            proj = _proj_mm(xb, lw["w_in"], tm=tm, tn=tn)
            y = _rg_core(proj.reshape(B, S, 2 * D), lw["conv_w"], lw["conv_b"], lw["w_a"], lw["b_a"],
                         lw["w_i"], lw["b_i"], lw["cneg"], ts=rg_ts)
            xf, xb = _res_mm(y.reshape(T, D), lw["w_out"], xf, tm=tm, tn=tn)
            h = _glu_mm(xb, lw["w_gate"], lw["w_up"], tm=tm, tn=tn)
        else:
            o, ob, gates = _pool_mix(xf.reshape(B, S, D), lw["mix_gain"], lw["pool_w"], lw["pool_scale"],
                                     lw["r_hi"], lw["r_lo"], lw["n_experts"], ts=pool_ts)
            xf, xb = o.reshape(T, D), ob.reshape(T, D)
            h = _moe_glu_mm(xb, lw["w_gate"], lw["w_up"], gates.reshape(T, LANES), tm=tm)
        xf, xb = _res_mm(h, lw["w_down"], xf, tm=tm, tn=tn)
    return _final_norm(xf, final_gain, tm=256).reshape(B, S, D)


def kernel(x_prompt, x_sample, mix_norm, ffn_norm, final_norm, rg_w_in, rg_conv_w, rg_conv_b, rg_w_a, rg_b_a, rg_w_i, rg_b_i, rg_lambda, rg_w_out, pool_w, pool_scale, ffn_w_gate, ffn_w_up, ffn_w_down, moe_router, moe_w_gate, moe_w_up, moe_w_down):
    layers, final_gain = _prep_weights(
        mix_norm, ffn_norm, final_norm, rg_w_in, rg_conv_w, rg_conv_b, rg_w_a, rg_b_a, rg_w_i, rg_b_i,
        rg_lambda, rg_w_out, pool_w, pool_scale, ffn_w_gate, ffn_w_up, ffn_w_down, moe_router,
        moe_w_gate, moe_w_up, moe_w_down)
    run = functools.partial(_trunk, layers=layers, final_gain=final_gain, tm=1024, tn=512, pool_ts=256)
    y_prompt = run(x_prompt, rg_ts=2048)
    y_sample = run(x_sample, rg_ts=1024)
    return (y_prompt, y_sample)
```

```python
import functools
import math

import jax
import jax.numpy as jnp
from jax import lax
from jax.experimental import pallas as pl
from jax.experimental.pallas import tpu as pltpu

NORM_EPS = 1e-6
RG_C = 8.0
RG_BLOCK = 256
CONV_WIDTH = 4
CONV_PAD_LEFT = (CONV_WIDTH - 1) // 2
POOL_WINDOWS = (2, 4, 8, 16)
TOP_K = 2
SUBLANES = 8
LANES = 128
HALO = 8
VMEM_LIMIT = 56 * 1024 * 1024
LOG2E = 1.4426950408889634
TINY = 1e-30

_BF16 = jnp.bfloat16
_F32 = jnp.float32


def _params(*sem):
    return pltpu.CompilerParams(dimension_semantics=sem, vmem_limit_bytes=VMEM_LIMIT)


def _store_row_rstd(rs_ref, x_ref):
    tm = x_ref.shape[0]
    rb = min(128, tm)

    def body(s, _):
        r0 = pl.multiple_of(s * rb, rb)
        xf = x_ref[pl.ds(r0, rb), :].astype(_F32)
        rs_ref[pl.ds(r0, rb), :] = lax.rsqrt(jnp.mean(xf * xf, axis=-1, keepdims=True) + NORM_EPS)
        return 0

    lax.fori_loop(0, tm // rb, body, 0)


def _sigmoid(x):
    return 1.0 / (1.0 + jnp.exp2(x * (-LOG2E)))


def _proj_kernel(x_ref, w_ref, o_ref, rs_ref):
    @pl.when(pl.program_id(1) == 0)
    def _():
        _store_row_rstd(rs_ref, x_ref)

    o_ref[...] = jnp.dot(x_ref[...], w_ref[...], preferred_element_type=_F32) * rs_ref[...]


def _proj_mm(xb, w, *, tm, tn):
    T, K = xb.shape
    N = w.shape[1]
    tm, tn = min(tm, T), min(tn, N)
    return pl.pallas_call(
        _proj_kernel,
        out_shape=jax.ShapeDtypeStruct((T, N), _F32),
        grid=(T // tm, N // tn),
        in_specs=[pl.BlockSpec((tm, K), lambda i, j: (i, 0)),
                  pl.BlockSpec((K, tn), lambda i, j: (0, j))],
        out_specs=pl.BlockSpec((tm, tn), lambda i, j: (i, j)),
        scratch_shapes=[pltpu.VMEM((tm, 1), _F32)],
        compiler_params=_params("parallel", "arbitrary"),
        name="proj_mm",
    )(xb, w)


def _res_kernel(y_ref, w_ref, r_ref, o_ref, ob_ref):
    o = r_ref[...] + jnp.dot(y_ref[...], w_ref[...], preferred_element_type=_F32)
    o_ref[...] = o
    ob_ref[...] = o.astype(_BF16)


def _res_mm(y, w, res, *, tm, tn):
    T, K = y.shape
    N = w.shape[1]
    tm, tn = min(tm, T), min(tn, N)
    return pl.pallas_call(
        _res_kernel,
        out_shape=(jax.ShapeDtypeStruct((T, N), _F32), jax.ShapeDtypeStruct((T, N), _BF16)),
        grid=(T // tm, N // tn),
        in_specs=[pl.BlockSpec((tm, K), lambda i, j: (i, 0)),
                  pl.BlockSpec((K, tn), lambda i, j: (0, j)),
                  pl.BlockSpec((tm, tn), lambda i, j: (i, j))],
        out_specs=(pl.BlockSpec((tm, tn), lambda i, j: (i, j)),
                   pl.BlockSpec((tm, tn), lambda i, j: (i, j))),
        compiler_params=_params("parallel", "parallel"),
        name="res_mm",
    )(y, w, res)


def _glu_kernel(x_ref, wg_ref, wu_ref, o_ref, rs_ref):
    @pl.when(pl.program_id(1) == 0)
    def _():
        _store_row_rstd(rs_ref, x_ref)

    rs = rs_ref[...]
    g = jnp.dot(x_ref[...], wg_ref[...], preferred_element_type=_F32) * rs
    u = jnp.dot(x_ref[...], wu_ref[...], preferred_element_type=_F32) * rs
    o_ref[...] = (g * _sigmoid(g) * u).astype(o_ref.dtype)


def _glu_mm(xb, wg, wu, *, tm, tn):
    T, K = xb.shape
    N = wg.shape[1]
    tm, tn = min(tm, T), min(tn, N)
    return pl.pallas_call(
        _glu_kernel,
        out_shape=jax.ShapeDtypeStruct((T, N), _BF16),
        grid=(T // tm, N // tn),
        in_specs=[pl.BlockSpec((tm, K), lambda i, j: (i, 0)),
                  pl.BlockSpec((K, tn), lambda i, j: (0, j)),
                  pl.BlockSpec((K, tn), lambda i, j: (0, j))],
        out_specs=pl.BlockSpec((tm, tn), lambda i, j: (i, j)),
        scratch_shapes=[pltpu.VMEM((tm, 1), _F32)],
        compiler_params=_params("parallel", "arbitrary"),
        name="glu_mm",
    )(xb, wg, wu)


def _moe_glu_kernel(x_ref, wg_ref, wu_ref, gt_ref, o_ref, rs_ref):
    e = pl.program_id(1)

    @pl.when(e == 0)
    def _():
        _store_row_rstd(rs_ref, x_ref)

    rs = rs_ref[...]
    g = jnp.dot(x_ref[...], wg_ref[...], preferred_element_type=_F32) * rs
    u = jnp.dot(x_ref[...], wu_ref[...], preferred_element_type=_F32) * rs
    gates = gt_ref[...]
    lane = lax.broadcasted_iota(jnp.int32, gates.shape, 1)
    ge = jnp.sum(jnp.where(lane == e, gates, 0.0), axis=-1, keepdims=True)
    o_ref[...] = (g * _sigmoid(g) * u * ge).astype(o_ref.dtype)


def _moe_glu_mm(xb, wg, wu, gates, *, tm):
    T, K = xb.shape
    E, _, F = wg.shape
    tm = min(tm, T)
    return pl.pallas_call(
        _moe_glu_kernel,
        out_shape=jax.ShapeDtypeStruct((T, E * F), _BF16),
        grid=(T // tm, E),
        in_specs=[pl.BlockSpec((tm, K), lambda i, e: (i, 0)),
                  pl.BlockSpec((None, K, F), lambda i, e: (e, 0, 0)),
                  pl.BlockSpec((None, K, F), lambda i, e: (e, 0, 0)),
                  pl.BlockSpec((tm, LANES), lambda i, e: (i, 0))],
        out_specs=pl.BlockSpec((tm, F), lambda i, e: (i, e)),
        scratch_shapes=[pltpu.VMEM((tm, 1), _F32)],
        compiler_params=_params("parallel", "arbitrary"),
        name="moe_glu_mm",
    )(xb, wg, wu, gates)


def _tile_scan(a, b, row, reverse):
    for d in (1, 2, 4):
        if reverse:
            shift, valid = SUBLANES - d, row < SUBLANES - d
        else:
            shift, valid = d, row >= d
        a_s = pltpu.roll(a, shift, 0)
        b_s = pltpu.roll(b, shift, 0)
        b = b + a * jnp.where(valid, b_s, 0.0)
        a = a * jnp.where(valid, a_s, 1.0)
    return a, b


def _rg_kernel(rec_ref, prev_ref, next_ref, gate_ref, cw_ref, cb_ref, wa_ref, ba_ref, wi_ref, bi_ref,
               cn_ref, y_ref, hs_ref, us_ref, ext_ref, a_ref, b_ref, carry_ref, *, ts, n_chunks, sb):
    p = pl.program_id(2)
    k = pl.program_id(3)
    t = jnp.where(p == 0, k, n_chunks - 1 - k)
    base = pl.multiple_of(t * ts, ts)
    C = rec_ref.shape[-1]
    L = sb + 2 * HALO
    row = lax.broadcasted_iota(jnp.int32, (SUBLANES, C), 0)
    n_tiles = ts // SUBLANES

    @pl.when(k == 0)
    def _():
        carry_ref[...] = jnp.zeros_like(carry_ref)

    ba = ba_ref[...]
    bi = bi_ref[...]
    ch = cn_ref[...] * (0.5 * LOG2E)

    def gates(u, r0):
        ub = u.astype(_BF16)
        ta = jnp.tanh(jnp.dot(ub, wa_ref[...], preferred_element_type=_F32) + ba)
        ti = jnp.tanh(jnp.dot(ub, wi_ref[...], preferred_element_type=_F32) + bi)
        a = jnp.exp2(ch + ch * ta)
        x = 1.0 - a * a
        a_ref[pl.ds(r0, sb), :] = a
        b_ref[pl.ds(r0, sb), :] = x * lax.rsqrt(jnp.maximum(x, TINY)) * (0.5 + 0.5 * ti) * u

    @pl.when(p == 0)
    def _():
        ext_ref[pl.ds(HALO, ts), :] = rec_ref[...]
        ext_ref[pl.ds(0, HALO), :] = jnp.where(t > 0, prev_ref[...], 0.0)
        ext_ref[pl.ds(HALO + ts, HALO), :] = jnp.where(t < n_chunks - 1, next_ref[...], 0.0)
        cw = cw_ref[...]
        cb = cb_ref[...]

        def gates_body(s, _):
            r0 = pl.multiple_of(s * sb, sb)
            e = ext_ref[pl.ds(r0, L), :]
            u = cb + cw[CONV_PAD_LEFT:CONV_PAD_LEFT + 1, :] * e[HALO:HALO + sb, :]
            for tap in range(CONV_WIDTH):
                d = tap - CONV_PAD_LEFT
                if d != 0:
                    u = u + cw[tap:tap + 1, :] * pltpu.roll(e, (-d) % L, 0)[HALO:HALO + sb, :]
            us_ref[pl.ds(base + r0, sb), :] = u
            gates(u, r0)
            return 0

        lax.fori_loop(0, ts // sb, gates_body, 0, unroll=2)

        def body(g, carry):
            r0 = pl.multiple_of(g * SUBLANES, SUBLANES)
            a, b = _tile_scan(a_ref[pl.ds(r0, SUBLANES), :], b_ref[pl.ds(r0, SUBLANES), :], row, False)
            h = b + a * carry
            hs_ref[pl.ds(base + r0, SUBLANES), :] = h
            return jnp.broadcast_to(h[SUBLANES - 1:SUBLANES, :], (SUBLANES, C))

        carry_ref[...] = lax.fori_loop(0, n_tiles, body, carry_ref[...], unroll=4)

    @pl.when(p == 1)
    def _():
        def gates_body(s, _):
            r0 = pl.multiple_of(s * sb, sb)
            gates(us_ref[pl.ds(base + r0, sb), :], r0)
            return 0

        lax.fori_loop(0, ts // sb, gates_body, 0, unroll=2)

        def body(g, carry):
            r0 = pl.multiple_of((n_tiles - 1 - g) * SUBLANES, SUBLANES)
            a, b = _tile_scan(a_ref[pl.ds(r0, SUBLANES), :], b_ref[pl.ds(r0, SUBLANES), :], row, True)
            h = b + a * carry
            ext_ref[pl.ds(r0, SUBLANES), :] = h + hs_ref[pl.ds(base + r0, SUBLANES), :]
            return jnp.broadcast_to(h[0:1, :], (SUBLANES, C))

        carry_ref[...] = lax.fori_loop(0, n_tiles, body, carry_ref[...], unroll=4)

        def out_body(s, _):
            r0 = pl.multiple_of(s * sb, sb)
            gate = gate_ref[pl.ds(r0, sb), :]
            y_ref[pl.ds(r0, sb), :] = (ext_ref[pl.ds(r0, sb), :] * jax.nn.gelu(gate)).astype(y_ref.dtype)
            return 0

        lax.fori_loop(0, ts // sb, out_body, 0)


def _rg_core(proj, conv_w, conv_b, w_a, b_a, w_i, b_i, cneg, *, ts):
    B, S, D2 = proj.shape
    D = D2 // 2
    C = RG_BLOCK
    nc = D // C
    ts = min(ts, S)
    n = S // ts
    sb = min(256, ts)
    tpb = ts // HALO

    def t_of(p, k):
        return jnp.where(p == 0, k, n - 1)

    def t_out(p, k):
        return jnp.where(p == 0, n - 1, n - 1 - k)

    kern = functools.partial(_rg_kernel, ts=ts, n_chunks=n, sb=sb)
    return pl.pallas_call(
        kern,
        out_shape=jax.ShapeDtypeStruct((B, S, D), _BF16),
        grid=(B, nc, 2, n),
        in_specs=[
            pl.BlockSpec((None, ts, C), lambda b, c, p, k: (b, t_of(p, k), nc + c)),
            pl.BlockSpec((None, HALO, C), lambda b, c, p, k: (b, jnp.maximum(t_of(p, k) * tpb - 1, 0), nc + c)),
            pl.BlockSpec((None, HALO, C),
                         lambda b, c, p, k: (b, jnp.minimum((t_of(p, k) + 1) * tpb, S // HALO - 1), nc + c)),
            pl.BlockSpec((None, ts, C), lambda b, c, p, k: (b, t_out(p, k), c)),
            pl.BlockSpec((CONV_WIDTH, C), lambda b, c, p, k: (0, c)),
            pl.BlockSpec((1, C), lambda b, c, p, k: (0, c)),
            pl.BlockSpec((None, None, C, C), lambda b, c, p, k: (p, c, 0, 0)),
            pl.BlockSpec((None, 1, C), lambda b, c, p, k: (p, 0, c)),
            pl.BlockSpec((None, None, C, C), lambda b, c, p, k: (p, c, 0, 0)),
            pl.BlockSpec((None, 1, C), lambda b, c, p, k: (p, 0, c)),
            pl.BlockSpec((None, 1, C), lambda b, c, p, k: (p, 0, c)),
        ],
        out_specs=pl.BlockSpec((None, ts, C), lambda b, c, p, k: (b, t_out(p, k), c)),
        scratch_shapes=[
            pltpu.VMEM((S, C), _F32),
            pltpu.VMEM((S, C), _F32),
            pltpu.VMEM((ts + 2 * HALO, C), _F32),
            pltpu.VMEM((ts, C), _F32),
            pltpu.VMEM((ts, C), _F32),
            pltpu.VMEM((SUBLANES, C), _F32),
        ],
        compiler_params=_params("parallel", "parallel", "arbitrary", "arbitrary"),
        name="rg_core",
    )(proj, proj, proj, proj, conv_w, conv_b, w_a, b_a, w_i, b_i, cneg)


def _pool_kernel(x_ref, prev_ref, next_ref, g_ref, pw_ref, ps_ref, rh_ref, rl_ref,
                 o_ref, ob_ref, gt_ref, ext_ref, *, ts, n_chunks, seq, n_experts):
    t = pl.program_id(1)
    D = x_ref.shape[-1]
    G = len(POOL_WINDOWS)
    gs = D // G
    L = ts + 2 * HALO

    def normed(v):
        return v * lax.rsqrt(jnp.mean(v * v, axis=-1, keepdims=True) + NORM_EPS) * g_ref[...]

    ext_ref[pl.ds(HALO, ts), :] = normed(x_ref[...])
    ext_ref[pl.ds(0, HALO), :] = jnp.where(t > 0, normed(prev_ref[...]), 0.0)
    ext_ref[pl.ds(HALO + ts, HALO), :] = jnp.where(t < n_chunks - 1, normed(next_ref[...]), 0.0)

    tpos = t * ts + lax.broadcasted_iota(jnp.int32, (ts, 1), 0)
    ssq = jnp.zeros((ts, 1), _F32)
    logits = jnp.zeros((ts, LANES), _F32)
    for gi, win in enumerate(POOL_WINDOWS):
        cols = pl.ds(gi * gs, gs)
        e = ext_ref[:, cols]
        s = pltpu.roll(e, 1, 0) + e
        w = 2
        while w < win:
            s = pltpu.roll(s, w // 2, 0) + pltpu.roll(s, L - w // 2, 0)
            w *= 2
        half = win // 2
        cnt = jnp.minimum(tpos + (win - half), seq) - jnp.maximum(tpos - half, 0)
        d = s[HALO:HALO + ts, :] * (1.0 / cnt.astype(_F32)) - e[HALO:HALO + ts, :]
        y = jnp.dot(d.astype(_BF16), pw_ref[gi], preferred_element_type=_F32)
        o = x_ref[:, cols] + y * ps_ref[:, cols]
        o_ref[:, cols] = o
        oh = o.astype(_BF16)
        ob_ref[:, cols] = oh
        ssq = ssq + jnp.sum(o * o, axis=-1, keepdims=True)
        ol = (o - oh.astype(_F32)).astype(_BF16)
        rh = rh_ref[cols, :]
        logits = (logits + jnp.dot(oh, rh, preferred_element_type=_F32)
                  + jnp.dot(oh, rl_ref[cols, :], preferred_element_type=_F32)
                  + jnp.dot(ol, rh, preferred_element_type=_F32))

    logits = logits * lax.rsqrt(ssq * (1.0 / D) + NORM_EPS)
    lane = lax.broadcasted_iota(jnp.int32, (ts, LANES), 1).astype(_F32)
    neg = jnp.float32(-jnp.inf)
    logits = jnp.where(lane < n_experts, logits, neg)
    m1 = jnp.max(logits, axis=-1, keepdims=True)
    i1 = jnp.min(jnp.where(logits == m1, lane, float(LANES)), axis=-1, keepdims=True)
    rest = jnp.where(lane == i1, neg, logits)
    m2 = jnp.max(rest, axis=-1, keepdims=True)
    i2 = jnp.min(jnp.where(rest == m2, lane, float(LANES)), axis=-1, keepdims=True)
    e2 = jnp.exp(m2 - m1)
    g1 = 1.0 / (1.0 + e2)
    gt_ref[...] = jnp.where(lane == i1, g1, 0.0) + jnp.where(lane == i2, e2 * g1, 0.0)


def _pool_mix(x, gain, pool_w, pool_scale, r_hi, r_lo, n_experts, *, ts):
    B, S, D = x.shape
    ts = min(ts, S)
    n = S // ts
    tpb = ts // HALO
    G, gs, _ = pool_w.shape
    kern = functools.partial(_pool_kernel, ts=ts, n_chunks=n, seq=S, n_experts=n_experts)
    o, ob, gt = pl.pallas_call(
        kern,
        out_shape=(jax.ShapeDtypeStruct((B, S, D), _F32), jax.ShapeDtypeStruct((B, S, D), _BF16),
                   jax.ShapeDtypeStruct((B, S, LANES), _F32)),
        grid=(B, n),
        in_specs=[
            pl.BlockSpec((None, ts, D), lambda b, t: (b, t, 0)),
            pl.BlockSpec((None, HALO, D), lambda b, t: (b, jnp.maximum(t * tpb - 1, 0), 0)),
            pl.BlockSpec((None, HALO, D), lambda b, t: (b, jnp.minimum((t + 1) * tpb, S // HALO - 1), 0)),
            pl.BlockSpec((1, D), lambda b, t: (0, 0)),
            pl.BlockSpec((G, gs, gs), lambda b, t: (0, 0, 0)),
            pl.BlockSpec((1, D), lambda b, t: (0, 0)),
            pl.BlockSpec((D, LANES), lambda b, t: (0, 0)),
            pl.BlockSpec((D, LANES), lambda b, t: (0, 0)),
        ],
        out_specs=(pl.BlockSpec((None, ts, D), lambda b, t: (b, t, 0)),
                   pl.BlockSpec((None, ts, D), lambda b, t: (b, t, 0)),
                   pl.BlockSpec((None, ts, LANES), lambda b, t: (b, t, 0))),
        scratch_shapes=[pltpu.VMEM((ts + 2 * HALO, D), _F32)],
        compiler_params=_params("parallel", "parallel"),
        name="pool_mix",
    )(x, x, x, gain, pool_w, pool_scale, r_hi, r_lo)
    return o, ob, gt


def _norm_kernel(x_ref, g_ref, o_ref):
    x = x_ref[...]
    o_ref[...] = x * lax.rsqrt(jnp.mean(x * x, axis=-1, keepdims=True) + NORM_EPS) * g_ref[...]


def _final_norm(x, gain, *, tm):
    T, D = x.shape
    tm = min(tm, T)
    return pl.pallas_call(
        _norm_kernel,
        out_shape=jax.ShapeDtypeStruct((T, D), _F32),
        grid=(T // tm,),
        in_specs=[pl.BlockSpec((tm, D), lambda i: (i, 0)), pl.BlockSpec((1, D), lambda i: (0, 0))],
        out_specs=pl.BlockSpec((tm, D), lambda i: (i, 0)),
        compiler_params=_params("parallel"),
        name="final_norm",
    )(x, gain)


def _fold(gain, w):
    return (gain[:, None] * w).astype(_BF16)


def _prep_weights(mix_norm, ffn_norm, final_norm, rg_w_in, rg_conv_w, rg_conv_b, rg_w_a, rg_b_a, rg_w_i,
                  rg_b_i, rg_lambda, rg_w_out, pool_w, pool_scale, ffn_w_gate, ffn_w_up, ffn_w_down,
                  moe_router, moe_w_gate, moe_w_up, moe_w_down):
    depth, D = mix_norm.shape
    layers = []
    for i in range(depth):
        j = i // 2
        lw = {}
        if i % 2 == 0:
            lw["w_in"] = _fold(mix_norm[i], rg_w_in[j])
            lw["conv_w"] = rg_conv_w[j]
            lw["conv_b"] = rg_conv_b[j][None, :]
            lw["w_a"] = (0.5 * rg_w_a[j]).astype(_BF16)
            lw["b_a"] = 0.5 * rg_b_a[j][:, None, :]
            lw["w_i"] = (0.5 * rg_w_i[j]).astype(_BF16)
            lw["b_i"] = 0.5 * rg_b_i[j][:, None, :]
            lw["cneg"] = (-RG_C * jax.nn.softplus(-rg_lambda[j]))[:, None, :]
            lw["w_out"] = rg_w_out[j].astype(_BF16)
            lw["w_gate"] = _fold(ffn_norm[i], ffn_w_gate[j])
            lw["w_up"] = _fold(ffn_norm[i], ffn_w_up[j])
            lw["w_down"] = ffn_w_down[j].astype(_BF16)
        else:
            E = moe_router.shape[-1]
            lw["mix_gain"] = mix_norm[i][None, :]
            lw["pool_w"] = pool_w[j].astype(_BF16)
            lw["pool_scale"] = pool_scale[j][None, :]
            router = jnp.pad(ffn_norm[i][:, None] * moe_router[j], ((0, 0), (0, LANES - E)))
            lw["r_hi"] = router.astype(_BF16)
            lw["r_lo"] = (router - lw["r_hi"].astype(_F32)).astype(_BF16)
            lw["n_experts"] = E
            lw["w_gate"] = _fold(ffn_norm[i], moe_w_gate[j])
            lw["w_up"] = _fold(ffn_norm[i], moe_w_up[j])
            lw["w_down"] = moe_w_down[j].reshape(-1, D).astype(_BF16)
        layers.append(lw)
    return layers, final_norm[None, :]


def _trunk(x, layers, final_gain, *, tm, tn, rg_ts, pool_ts):
    B, S, D = x.shape
    T = B * S
    xf = x.reshape(T, D)
    xb = xf.astype(_BF16)
    for i, lw in enumerate(layers):
        if i % 2 == 0:
            proj = _proj_mm(xb, lw["w_in"], tm=tm, tn=tn)
            y = _rg_core(proj.reshape(B, S, 2 * D), lw["conv_w"], lw["conv_b"], lw["w_a"], lw["b_a"],
                         lw["w_i"], lw["b_i"], lw["cneg"], ts=rg_ts)
            xf, xb = _res_mm(y.reshape(T, D), lw["w_out"], xf, tm=tm, tn=tn)
            h = _glu_mm(xb, lw["w_gate"], lw["w_up"], tm=tm, tn=tn)
        else:
            o, ob, gates = _pool_mix(xf.reshape(B, S, D), lw["mix_gain"], lw["pool_w"], lw["pool_scale"],
                                     lw["r_hi"], lw["r_lo"], lw["n_experts"], ts=pool_ts)
            xf, xb = o.reshape(T, D), ob.reshape(T, D)
            h = _moe_glu_mm(xb, lw["w_gate"], lw["w_up"], gates.reshape(T, LANES), tm=tm)
        xf, xb = _res_mm(h, lw["w_down"], xf, tm=tm, tn=tn)
    return _final_norm(xf, final_gain, tm=256).reshape(B, S, D)


def kernel(x_prompt, x_sample, mix_norm, ffn_norm, final_norm, rg_w_in, rg_conv_w, rg_conv_b, rg_w_a, rg_b_a, rg_w_i, rg_b_i, rg_lambda, rg_w_out, pool_w, pool_scale, ffn_w_gate, ffn_w_up, ffn_w_down, moe_router, moe_w_gate, moe_w_up, moe_w_down):
    layers, final_gain = _prep_weights(
        mix_norm, ffn_norm, final_norm, rg_w_in, rg_conv_w, rg_conv_b, rg_w_a, rg_b_a, rg_w_i, rg_b_i,
        rg_lambda, rg_w_out, pool_w, pool_scale, ffn_w_gate, ffn_w_up, ffn_w_down, moe_router,
        moe_w_gate, moe_w_up, moe_w_down)
    run = functools.partial(_trunk, layers=layers, final_gain=final_gain, tm=1024, tn=512, pool_ts=256)
    y_prompt = run(x_prompt, rg_ts=2048)
    y_sample = run(x_sample, rg_ts=1024)
    return (y_prompt, y_sample)
```

```python
import functools
import math

import jax
import jax.numpy as jnp
from jax import lax
from jax.experimental import pallas as pl
from jax.experimental.pallas import tpu as pltpu

NORM_EPS = 1e-6
RG_C = 8.0
RG_BLOCK = 256
CONV_WIDTH = 4
CONV_PAD_LEFT = (CONV_WIDTH - 1) // 2
POOL_WINDOWS = (2, 4, 8, 16)
TOP_K = 2
SUBLANES = 8
LANES = 128
HALO = 8
VMEM_LIMIT = 56 * 1024 * 1024
LOG2E = 1.4426950408889634
GELU_C1 = math.sqrt(2.0 / math.pi)
GELU_C3 = 0.044715 * GELU_C1
TINY = 1e-30
SCAN_LC = 64

_BF16 = jnp.bfloat16
_F32 = jnp.float32


def _tiles(seq):
    return dict(tm=1024, tn=512, rg_ts=2048 if seq <= 2048 else 1024, pool_ts=256)


def _params(*sem):
    return pltpu.CompilerParams(dimension_semantics=sem, vmem_limit_bytes=VMEM_LIMIT)


def _store_row_rstd(rs_ref, x_ref):
    tm = x_ref.shape[0]
    rb = min(128, tm)

    def body(s, _):
        r0 = pl.multiple_of(s * rb, rb)
        xf = x_ref[pl.ds(r0, rb), :].astype(_F32)
        rs_ref[pl.ds(r0, rb), :] = lax.rsqrt(jnp.mean(xf * xf, axis=-1, keepdims=True) + NORM_EPS)
        return 0

    lax.fori_loop(0, tm // rb, body, 0)


def _sigmoid(x):
    return 1.0 / (1.0 + jnp.exp2(x * (-LOG2E)))


def _gelu_tanh(x):
    inner = x * (GELU_C1 + GELU_C3 * (x * x))
    return (0.5 * x) * (1.0 + jnp.tanh(inner))


def _proj_kernel(x_ref, w_ref, o_ref, rs_ref):
    @pl.when(pl.program_id(1) == 0)
    def _():
        _store_row_rstd(rs_ref, x_ref)

    o_ref[...] = jnp.dot(x_ref[...], w_ref[...], preferred_element_type=_F32) * rs_ref[...]


def _proj_mm(xb, w, *, tm, tn):
    T, K = xb.shape
    N = w.shape[1]
    tm, tn = min(tm, T), min(tn, N)
    return pl.pallas_call(
        _proj_kernel,
        out_shape=jax.ShapeDtypeStruct((T, N), _F32),
        grid=(T // tm, N // tn),
        in_specs=[pl.BlockSpec((tm, K), lambda i, j: (i, 0)),
                  pl.BlockSpec((K, tn), lambda i, j: (0, j))],
        out_specs=pl.BlockSpec((tm, tn), lambda i, j: (i, j)),
        scratch_shapes=[pltpu.VMEM((tm, 1), _F32)],
        compiler_params=_params("parallel", "arbitrary"),
        name="proj_mm",
    )(xb, w)


def _res_kernel(y_ref, w_ref, r_ref, o_ref, ob_ref):
    o = r_ref[...] + jnp.dot(y_ref[...], w_ref[...], preferred_element_type=_F32)
    o_ref[...] = o
    ob_ref[...] = o.astype(_BF16)


def _res_mm(y, w, res, *, tm, tn):
    T, K = y.shape
    N = w.shape[1]
    tm, tn = min(tm, T), min(tn, N)
    return pl.pallas_call(
        _res_kernel,
        out_shape=(jax.ShapeDtypeStruct((T, N), _F32), jax.ShapeDtypeStruct((T, N), _BF16)),
        grid=(T // tm, N // tn),
        in_specs=[pl.BlockSpec((tm, K), lambda i, j: (i, 0)),
                  pl.BlockSpec((K, tn), lambda i, j: (0, j)),
                  pl.BlockSpec((tm, tn), lambda i, j: (i, j))],
        out_specs=(pl.BlockSpec((tm, tn), lambda i, j: (i, j)),
                   pl.BlockSpec((tm, tn), lambda i, j: (i, j))),
        compiler_params=_params("parallel", "parallel"),
        name="res_mm",
    )(y, w, res)


def _glu_kernel(x_ref, wg_ref, wu_ref, o_ref, rs_ref):
    @pl.when(pl.program_id(1) == 0)
    def _():
        _store_row_rstd(rs_ref, x_ref)

    rs = rs_ref[...]
    g = jnp.dot(x_ref[...], wg_ref[...], preferred_element_type=_F32) * rs
    u = jnp.dot(x_ref[...], wu_ref[...], preferred_element_type=_F32) * rs
    o_ref[...] = (g * _sigmoid(g) * u).astype(o_ref.dtype)


def _glu_mm(xb, wg, wu, *, tm, tn):
    T, K = xb.shape
    N = wg.shape[1]
    tm, tn = min(tm, T), min(tn, N)
    return pl.pallas_call(
        _glu_kernel,
        out_shape=jax.ShapeDtypeStruct((T, N), _BF16),
        grid=(T // tm, N // tn),
        in_specs=[pl.BlockSpec((tm, K), lambda i, j: (i, 0)),
                  pl.BlockSpec((K, tn), lambda i, j: (0, j)),
                  pl.BlockSpec((K, tn), lambda i, j: (0, j))],
        out_specs=pl.BlockSpec((tm, tn), lambda i, j: (i, j)),
        scratch_shapes=[pltpu.VMEM((tm, 1), _F32)],
        compiler_params=_params("parallel", "arbitrary"),
        name="glu_mm",
    )(xb, wg, wu)


def _moe_glu_kernel(x_ref, wg_ref, wu_ref, gt_ref, o_ref, rs_ref):
    e = pl.program_id(1)

    @pl.when(e == 0)
    def _():
        _store_row_rstd(rs_ref, x_ref)

    rs = rs_ref[...]
    g = jnp.dot(x_ref[...], wg_ref[...], preferred_element_type=_F32) * rs
    u = jnp.dot(x_ref[...], wu_ref[...], preferred_element_type=_F32) * rs
    gates = gt_ref[...]
    lane = lax.broadcasted_iota(jnp.int32, gates.shape, 1)
    ge = jnp.sum(jnp.where(lane == e, gates, 0.0), axis=-1, keepdims=True)
    o_ref[...] = (g * _sigmoid(g) * u * ge).astype(o_ref.dtype)


def _moe_glu_mm(xb, wg, wu, gates, *, tm):
    T, K = xb.shape
    E, _, F = wg.shape
    tm = min(tm, T)
    return pl.pallas_call(
        _moe_glu_kernel,
        out_shape=jax.ShapeDtypeStruct((T, E * F), _BF16),
        grid=(T // tm, E),
        in_specs=[pl.BlockSpec((tm, K), lambda i, e: (i, 0)),
                  pl.BlockSpec((None, K, F), lambda i, e: (e, 0, 0)),
                  pl.BlockSpec((None, K, F), lambda i, e: (e, 0, 0)),
                  pl.BlockSpec((tm, LANES), lambda i, e: (i, 0))],
        out_specs=pl.BlockSpec((tm, F), lambda i, e: (i, e)),
        scratch_shapes=[pltpu.VMEM((tm, 1), _F32)],
        compiler_params=_params("parallel", "arbitrary"),
        name="moe_glu_mm",
    )(xb, wg, wu, gates)


def _rg_kernel(rec_ref, prev_ref, next_ref, gate_ref, cw_ref, cb_ref, wa_ref, ba_ref, wi_ref, bi_ref,
               cn_ref, y_ref, hs_ref, us_ref, ext_ref, a_ref, b_ref, hl_ref, pl_ref, carry_ref,
               *, ts, n_chunks, lc, sb):
    p = pl.program_id(2)
    k = pl.program_id(3)
    t = jnp.where(p == 0, k, n_chunks - 1 - k)
    base = pl.multiple_of(t * ts, ts)
    C = rec_ref.shape[-1]
    G = C // LANES
    R = ts // (SUBLANES * lc)
    L = sb + 2 * HALO
    per_sb = sb // lc
    row = lax.broadcasted_iota(jnp.int32, (SUBLANES, LANES), 0)
    streams = [(r, g) for r in range(R) for g in range(G)]

    @pl.when(k == 0)
    def _():
        carry_ref[...] = jnp.zeros_like(carry_ref)

    ba = ba_ref[...]
    bi = bi_ref[...]
    ch = cn_ref[...] * (0.5 * LOG2E)

    def gates(uh, s):
        ub = uh.astype(_BF16)
        ta = jnp.tanh(jnp.dot(ub, wa_ref[...], preferred_element_type=_F32) + ba)
        ti = jnp.tanh(jnp.dot(ub, wi_ref[...], preferred_element_type=_F32) + bi)
        a = jnp.exp2(ch + ch * ta)
        x = 1.0 - a * a
        b = x * lax.rsqrt(jnp.maximum(x, TINY)) * (1.0 + ti) * uh
        for i in range(per_sb):
            for g in range(G):
                a_ref[piece(g, s * per_sb + i)] = a[i * lc:(i + 1) * lc, g * LANES:(g + 1) * LANES]
                b_ref[piece(g, s * per_sb + i)] = b[i * lc:(i + 1) * lc, g * LANES:(g + 1) * LANES]

    def piece(g, q):
        r, j = q // SUBLANES, q % SUBLANES
        return (g, pl.ds(r * (lc * SUBLANES) + j, lc, stride=SUBLANES), slice(None))

    def tile(g, r, k8):
        return (g, pl.ds(k8 + r * (lc * SUBLANES), SUBLANES), slice(None))

    def local_scan(reverse):
        def body(i, hp):
            k8 = pl.multiple_of(((lc - 1 - i) if reverse else i) * SUBLANES, SUBLANES)
            out = []
            for n, (r, g) in enumerate(streams):
                at = a_ref[tile(g, r, k8)]
                h = at * hp[2 * n] + b_ref[tile(g, r, k8)]
                pr = at * hp[2 * n + 1]
                hl_ref[tile(g, r, k8)] = h
                pl_ref[tile(g, r, k8)] = pr
                out += [h, pr]
            return tuple(out)

        init = (jnp.zeros((SUBLANES, LANES), _F32), jnp.ones((SUBLANES, LANES), _F32)) * len(streams)
        return lax.fori_loop(0, lc, body, init, unroll=2)

    def sub_chunk_states(hp, reverse):
        cvecs = {}
        for g in range(G):
            cur = carry_ref[g]
            for r in (range(R - 1, -1, -1) if reverse else range(R)):
                n = streams.index((r, g))
                h_end, p_end = hp[2 * n], hp[2 * n + 1]
                cvec = jnp.zeros((SUBLANES, LANES), _F32)
                for j in (range(SUBLANES - 1, -1, -1) if reverse else range(SUBLANES)):
                    cvec = jnp.where(row == j, cur, cvec)
                    nxt = h_end + p_end * cur
                    cur = jnp.broadcast_to(nxt[j:j + 1, :], (SUBLANES, LANES))
                cvecs[(r, g)] = cvec
            carry_ref[g] = cur
        return cvecs

    @pl.when(p == 0)
    def _():
        ext_ref[pl.ds(HALO, ts), :] = rec_ref[...]
        ext_ref[pl.ds(0, HALO), :] = jnp.where(t > 0, prev_ref[...], 0.0)
        ext_ref[pl.ds(HALO + ts, HALO), :] = jnp.where(t < n_chunks - 1, next_ref[...], 0.0)
        cw = cw_ref[...]
        cb = cb_ref[...]

        def gates_body(s, _):
            r0 = pl.multiple_of(s * sb, sb)
            e = ext_ref[pl.ds(r0, L), :]
            uh = cb + cw[CONV_PAD_LEFT:CONV_PAD_LEFT + 1, :] * e[HALO:HALO + sb, :]
            for tap in range(CONV_WIDTH):
                d = tap - CONV_PAD_LEFT
                if d != 0:
                    uh = uh + cw[tap:tap + 1, :] * pltpu.roll(e, (-d) % L, 0)[HALO:HALO + sb, :]
            us_ref[pl.ds(base + r0, sb), :] = uh
            gates(uh, s)
            return 0

        lax.fori_loop(0, ts // sb, gates_body, 0, unroll=2)
        cvecs = sub_chunk_states(local_scan(False), False)

        def fix_body(kk, _):
            k8 = pl.multiple_of(kk * SUBLANES, SUBLANES)
            for r, g in streams:
                hs_ref[tile(g, r, base + k8)] = hl_ref[tile(g, r, k8)] + pl_ref[tile(g, r, k8)] * cvecs[(r, g)]
            return 0

        lax.fori_loop(0, lc, fix_body, 0, unroll=2)

    @pl.when(p == 1)
    def _():
        def gates_body(s, _):
            r0 = pl.multiple_of(s * sb, sb)
            gates(us_ref[pl.ds(base + r0, sb), :], s)
            return 0

        lax.fori_loop(0, ts // sb, gates_body, 0, unroll=4)
        cvecs = sub_chunk_states(local_scan(True), True)

        def fix_body(kk, _):
            k8 = pl.multiple_of(kk * SUBLANES, SUBLANES)
            for r, g in streams:
                h = hl_ref[tile(g, r, k8)] + pl_ref[tile(g, r, k8)] * cvecs[(r, g)]
                a_ref[tile(g, r, k8)] = h + hs_ref[tile(g, r, base + k8)]
            return 0

        lax.fori_loop(0, lc, fix_body, 0, unroll=2)

        def out_body(s, _):
            r0 = pl.multiple_of(s * sb, sb)
            gate = _gelu_tanh(gate_ref[pl.ds(r0, sb), :])
            for i in range(per_sb):
                for g in range(G):
                    cols = slice(g * LANES, (g + 1) * LANES)
                    y_ref[pl.ds(r0 + i * lc, lc), cols] = (
                        a_ref[piece(g, s * per_sb + i)] * gate[i * lc:(i + 1) * lc, cols]).astype(y_ref.dtype)
            return 0

        lax.fori_loop(0, ts // sb, out_body, 0)


def _rg_core(proj, conv_w, conv_b, w_a, b_a, w_i, b_i, cneg, *, ts, lc=SCAN_LC):
    B, S, D2 = proj.shape
    D = D2 // 2
    C = RG_BLOCK
    G = C // LANES
    nc = D // C
    ts = min(ts, S)
    n = S // ts
    lc = min(lc, ts // SUBLANES)
    sb = max(min(256, ts), lc)
    assert lc % SUBLANES == 0 and ts % (SUBLANES * lc) == 0 and sb % lc == 0 and ts % sb == 0
    tpb = ts // HALO

    def t_of(p, k):
        return jnp.where(p == 0, k, n - 1)

    def t_out(p, k):
        return jnp.where(p == 0, n - 1, n - 1 - k)

    kern = functools.partial(_rg_kernel, ts=ts, n_chunks=n, lc=lc, sb=sb)
    return pl.pallas_call(
        kern,
        out_shape=jax.ShapeDtypeStruct((B, S, D), _BF16),
        grid=(B, nc, 2, n),
        in_specs=[
            pl.BlockSpec((None, ts, C), lambda b, c, p, k: (b, t_of(p, k), nc + c)),
            pl.BlockSpec((None, HALO, C), lambda b, c, p, k: (b, jnp.maximum(t_of(p, k) * tpb - 1, 0), nc + c)),
            pl.BlockSpec((None, HALO, C),
                         lambda b, c, p, k: (b, jnp.minimum((t_of(p, k) + 1) * tpb, S // HALO - 1), nc + c)),
            pl.BlockSpec((None, ts, C), lambda b, c, p, k: (b, t_out(p, k), c)),
            pl.BlockSpec((CONV_WIDTH, C), lambda b, c, p, k: (0, c)),
            pl.BlockSpec((1, C), lambda b, c, p, k: (0, c)),
            pl.BlockSpec((None, None, C, C), lambda b, c, p, k: (p, c, 0, 0)),
            pl.BlockSpec((None, 1, C), lambda b, c, p, k: (p, 0, c)),
            pl.BlockSpec((None, None, C, C), lambda b, c, p, k: (p, c, 0, 0)),
            pl.BlockSpec((None, 1, C), lambda b, c, p, k: (p, 0, c)),
            pl.BlockSpec((None, 1, C), lambda b, c, p, k: (p, 0, c)),
        ],
        out_specs=pl.BlockSpec((None, ts, C), lambda b, c, p, k: (b, t_out(p, k), c)),
        scratch_shapes=[
            pltpu.VMEM((G, S, LANES), _F32),
            pltpu.VMEM((S, C), _F32),
            pltpu.VMEM((ts + 2 * HALO, C), _F32),
            pltpu.VMEM((G, ts, LANES), _F32),
            pltpu.VMEM((G, ts, LANES), _F32),
            pltpu.VMEM((G, ts, LANES), _F32),
            pltpu.VMEM((G, ts, LANES), _F32),
            pltpu.VMEM((G, SUBLANES, LANES), _F32),
        ],
        compiler_params=_params("parallel", "parallel", "arbitrary", "arbitrary"),
        name="rg_core",
    )(proj, proj, proj, proj, conv_w, conv_b, w_a, b_a, w_i, b_i, cneg)


def _pool_kernel(x_ref, prev_ref, next_ref, g_ref, pw_ref, ps_ref, rh_ref, rl_ref,
                 o_ref, ob_ref, gt_ref, ext_ref, *, ts, n_chunks, seq, n_experts):
    t = pl.program_id(1)
    D = x_ref.shape[-1]
    G = len(POOL_WINDOWS)
    gs = D // G
    L = ts + 2 * HALO

    def normed(v):
        return v * lax.rsqrt(jnp.mean(v * v, axis=-1, keepdims=True) + NORM_EPS) * g_ref[...]

    ext_ref[pl.ds(HALO, ts), :] = normed(x_ref[...])
    ext_ref[pl.ds(0, HALO), :] = jnp.where(t > 0, normed(prev_ref[...]), 0.0)
    ext_ref[pl.ds(HALO + ts, HALO), :] = jnp.where(t < n_chunks - 1, normed(next_ref[...]), 0.0)

    tpos = t * ts + lax.broadcasted_iota(jnp.int32, (ts, 1), 0)
    ssq = jnp.zeros((ts, 1), _F32)
    logits = jnp.zeros((ts, LANES), _F32)
    for gi, win in enumerate(POOL_WINDOWS):
        cols = pl.ds(gi * gs, gs)
        e = ext_ref[:, cols]
        s = pltpu.roll(e, 1, 0) + e
        w = 2
        while w < win:
            s = pltpu.roll(s, w // 2, 0) + pltpu.roll(s, L - w // 2, 0)
            w *= 2
        half = win // 2
        cnt = jnp.minimum(tpos + (win - half), seq) - jnp.maximum(tpos - half, 0)
        d = s[HALO:HALO + ts, :] * (1.0 / cnt.astype(_F32)) - e[HALO:HALO + ts, :]
        y = jnp.dot(d.astype(_BF16), pw_ref[gi], preferred_element_type=_F32)
        o = x_ref[:, cols] + y * ps_ref[:, cols]
        o_ref[:, cols] = o
        oh = o.astype(_BF16)
        ob_ref[:, cols] = oh
        ssq = ssq + jnp.sum(o * o, axis=-1, keepdims=True)
        ol = (o - oh.astype(_F32)).astype(_BF16)
        rh = rh_ref[cols, :]
        logits = (logits + jnp.dot(oh, rh, preferred_element_type=_F32)
                  + jnp.dot(oh, rl_ref[cols, :], preferred_element_type=_F32)
                  + jnp.dot(ol, rh, preferred_element_type=_F32))

    logits = logits * lax.rsqrt(ssq * (1.0 / D) + NORM_EPS)
    lane = lax.broadcasted_iota(jnp.int32, (ts, LANES), 1).astype(_F32)
    neg = jnp.float32(-jnp.inf)
    logits = jnp.where(lane < n_experts, logits, neg)
    m1 = jnp.max(logits, axis=-1, keepdims=True)
    i1 = jnp.min(jnp.where(logits == m1, lane, float(LANES)), axis=-1, keepdims=True)
    rest = jnp.where(lane == i1, neg, logits)
    m2 = jnp.max(rest, axis=-1, keepdims=True)
    i2 = jnp.min(jnp.where(rest == m2, lane, float(LANES)), axis=-1, keepdims=True)
    e2 = jnp.exp(m2 - m1)
    g1 = 1.0 / (1.0 + e2)
    gt_ref[...] = jnp.where(lane == i1, g1, 0.0) + jnp.where(lane == i2, e2 * g1, 0.0)


def _pool_mix(x, gain, pool_w, pool_scale, r_hi, r_lo, n_experts, *, ts):
    B, S, D = x.shape
    ts = min(ts, S)
    n = S // ts
    tpb = ts // HALO
    G, gs, _ = pool_w.shape
    kern = functools.partial(_pool_kernel, ts=ts, n_chunks=n, seq=S, n_experts=n_experts)
    o, ob, gt = pl.pallas_call(
        kern,
        out_shape=(jax.ShapeDtypeStruct((B, S, D), _F32), jax.ShapeDtypeStruct((B, S, D), _BF16),
                   jax.ShapeDtypeStruct((B, S, LANES), _F32)),
        grid=(B, n),
        in_specs=[
            pl.BlockSpec((None, ts, D), lambda b, t: (b, t, 0)),
            pl.BlockSpec((None, HALO, D), lambda b, t: (b, jnp.maximum(t * tpb - 1, 0), 0)),
            pl.BlockSpec((None, HALO, D), lambda b, t: (b, jnp.minimum((t + 1) * tpb, S // HALO - 1), 0)),
            pl.BlockSpec((1, D), lambda b, t: (0, 0)),
            pl.BlockSpec((G, gs, gs), lambda b, t: (0, 0, 0)),
            pl.BlockSpec((1, D), lambda b, t: (0, 0)),
            pl.BlockSpec((D, LANES), lambda b, t: (0, 0)),
            pl.BlockSpec((D, LANES), lambda b, t: (0, 0)),
        ],
        out_specs=(pl.BlockSpec((None, ts, D), lambda b, t: (b, t, 0)),
                   pl.BlockSpec((None, ts, D), lambda b, t: (b, t, 0)),
                   pl.BlockSpec((None, ts, LANES), lambda b, t: (b, t, 0))),
        scratch_shapes=[pltpu.VMEM((ts + 2 * HALO, D), _F32)],
        compiler_params=_params("parallel", "parallel"),
        name="pool_mix",
    )(x, x, x, gain, pool_w, pool_scale, r_hi, r_lo)
    return o, ob, gt


def _res_norm_kernel(y_ref, w_ref, r_ref, g_ref, o_ref, *, tn, rb):
    j = pl.program_id(1)
    c0 = pl.multiple_of(j * tn, tn)
    o_ref[:, pl.ds(c0, tn)] = r_ref[...] + jnp.dot(y_ref[...], w_ref[...], preferred_element_type=_F32)

    @pl.when(j == pl.num_programs(1) - 1)
    def _():
        def body(s, _):
            r0 = pl.multiple_of(s * rb, rb)
            x = o_ref[pl.ds(r0, rb), :]
            o_ref[pl.ds(r0, rb), :] = (x * lax.rsqrt(jnp.mean(x * x, axis=-1, keepdims=True) + NORM_EPS)
                                       * g_ref[...])
            return 0

        lax.fori_loop(0, o_ref.shape[0] // rb, body, 0)


def _res_norm_mm(y, w, res, gain, *, tm, tn):
    T, K = y.shape
    N = w.shape[1]
    tm, tn = min(tm, T), min(tn, N)
    rb = min(64, tm)
    return pl.pallas_call(
        functools.partial(_res_norm_kernel, tn=tn, rb=rb),
        out_shape=jax.ShapeDtypeStruct((T, N), _F32),
        grid=(T // tm, N // tn),
        in_specs=[pl.BlockSpec((tm, K), lambda i, j: (i, 0)),
                  pl.BlockSpec((K, tn), lambda i, j: (0, j)),
                  pl.BlockSpec((tm, tn), lambda i, j: (i, j)),
                  pl.BlockSpec((1, N), lambda i, j: (0, 0))],
        out_specs=pl.BlockSpec((tm, N), lambda i, j: (i, 0)),
        compiler_params=_params("parallel", "arbitrary"),
        name="res_norm_mm",
    )(y, w, res, gain)


def _fold(gain, w):
    return (gain[:, None] * w).astype(_BF16)


def _prep_weights(mix_norm, ffn_norm, final_norm, rg_w_in, rg_conv_w, rg_conv_b, rg_w_a, rg_b_a, rg_w_i,
                  rg_b_i, rg_lambda, rg_w_out, pool_w, pool_scale, ffn_w_gate, ffn_w_up, ffn_w_down,
                  moe_router, moe_w_gate, moe_w_up, moe_w_down):
    depth, D = mix_norm.shape
    layers = []
    for i in range(depth):
        j = i // 2
        lw = {}
        if i % 2 == 0:
            lw["w_in"] = _fold(mix_norm[i], rg_w_in[j])
            lw["conv_w"] = 0.5 * rg_conv_w[j]
            lw["conv_b"] = 0.5 * rg_conv_b[j][None, :]
            lw["w_a"] = rg_w_a[j].astype(_BF16)
            lw["b_a"] = 0.5 * rg_b_a[j][:, None, :]
            lw["w_i"] = rg_w_i[j].astype(_BF16)
            lw["b_i"] = 0.5 * rg_b_i[j][:, None, :]
            lw["cneg"] = (-RG_C * jax.nn.softplus(-rg_lambda[j]))[:, None, :]
            lw["w_out"] = rg_w_out[j].astype(_BF16)
            lw["w_gate"] = _fold(ffn_norm[i], ffn_w_gate[j])
            lw["w_up"] = _fold(ffn_norm[i], ffn_w_up[j])
            lw["w_down"] = ffn_w_down[j].astype(_BF16)
        else:
            E = moe_router.shape[-1]
            lw["mix_gain"] = mix_norm[i][None, :]
            lw["pool_w"] = pool_w[j].astype(_BF16)
            lw["pool_scale"] = pool_scale[j][None, :]
            router = jnp.pad(ffn_norm[i][:, None] * moe_router[j], ((0, 0), (0, LANES - E)))
            lw["r_hi"] = router.astype(_BF16)
            lw["r_lo"] = (router - lw["r_hi"].astype(_F32)).astype(_BF16)
            lw["n_experts"] = E
            lw["w_gate"] = _fold(ffn_norm[i], moe_w_gate[j])
            lw["w_up"] = _fold(ffn_norm[i], moe_w_up[j])
            lw["w_down"] = moe_w_down[j].reshape(-1, D).astype(_BF16)
        layers.append(lw)
    return layers, final_norm[None, :]


def _trunk(x, layers, final_gain, *, tm, tn, rg_ts, pool_ts):
    B, S, D = x.shape
    T = B * S
    xf = x.reshape(T, D)
    xb = xf.astype(_BF16)
    for i, lw in enumerate(layers):
        if i % 2 == 0:
            proj = _proj_mm(xb, lw["w_in"], tm=tm, tn=tn)
            y = _rg_core(proj.reshape(B, S, 2 * D), lw["conv_w"], lw["conv_b"], lw["w_a"], lw["b_a"],
                         lw["w_i"], lw["b_i"], lw["cneg"], ts=rg_ts)
            xf, xb = _res_mm(y.reshape(T, D), lw["w_out"], xf, tm=tm, tn=tn)
            h = _glu_mm(xb, lw["w_gate"], lw["w_up"], tm=tm, tn=tn)
        else:
            o, ob, gates = _pool_mix(xf.reshape(B, S, D), lw["mix_gain"], lw["pool_w"], lw["pool_scale"],
                                     lw["r_hi"], lw["r_lo"], lw["n_experts"], ts=pool_ts)
            xf, xb = o.reshape(T, D), ob.reshape(T, D)
            h = _moe_glu_mm(xb, lw["w_gate"], lw["w_up"], gates.reshape(T, LANES), tm=tm)
        if i + 1 < len(layers):
            xf, xb = _res_mm(h, lw["w_down"], xf, tm=tm, tn=tn)
    return _res_norm_mm(h, lw["w_down"], xf, final_gain, tm=tm // 2, tn=tn).reshape(B, S, D)


def kernel(x_prompt, x_sample, mix_norm, ffn_norm, final_norm, rg_w_in, rg_conv_w, rg_conv_b, rg_w_a, rg_b_a, rg_w_i, rg_b_i, rg_lambda, rg_w_out, pool_w, pool_scale, ffn_w_gate, ffn_w_up, ffn_w_down, moe_router, moe_w_gate, moe_w_up, moe_w_down):
    layers, final_gain = _prep_weights(
        mix_norm, ffn_norm, final_norm, rg_w_in, rg_conv_w, rg_conv_b, rg_w_a, rg_b_a, rg_w_i, rg_b_i,
        rg_lambda, rg_w_out, pool_w, pool_scale, ffn_w_gate, ffn_w_up, ffn_w_down, moe_router,
        moe_w_gate, moe_w_up, moe_w_down)
    y_prompt = _trunk(x_prompt, layers, final_gain, **_tiles(x_prompt.shape[1]))
    y_sample = _trunk(x_sample, layers, final_gain, **_tiles(x_sample.shape[1]))
    return (y_prompt, y_sample)
```

```python
import functools
import math

import jax
import jax.numpy as jnp
from jax import lax
from jax.experimental import pallas as pl
from jax.experimental.pallas import tpu as pltpu

NORM_EPS = 1e-6
RG_C = 8.0
RG_BLOCK = 256
CONV_WIDTH = 4
CONV_PAD_LEFT = (CONV_WIDTH - 1) // 2
POOL_WINDOWS = (2, 4, 8, 16)
TOP_K = 2
SUBLANES = 8
LANES = 128
HALO = 8
VMEM_LIMIT = 56 * 1024 * 1024
LOG2E = 1.4426950408889634
GELU_C1 = math.sqrt(2.0 / math.pi)
GELU_C3 = 0.044715 * GELU_C1
TINY = 1e-30
SCAN_LC = 64

_BF16 = jnp.bfloat16
_F32 = jnp.float32


def _tiles(seq):
    return dict(tm=1024, tn=512, rg_ts=2048 if seq <= 2048 else 1024, pool_ts=256)


def _params(*sem):
    return pltpu.CompilerParams(dimension_semantics=sem, vmem_limit_bytes=VMEM_LIMIT)


def _rstd_kernel(x_ref, rs_ref):
    xf = x_ref[...].astype(_F32)
    rs = lax.rsqrt(jnp.mean(xf * xf, axis=-1, keepdims=True) + NORM_EPS)
    rs_ref[...] = jnp.broadcast_to(rs, rs_ref.shape)


def _scale_rows(v, rs):
    return v * pltpu.repeat(rs, v.shape[1] // LANES, axis=1)


def _row_rstd(xb, *, tm):
    T, K = xb.shape
    tm = min(tm, T)
    return pl.pallas_call(
        _rstd_kernel,
        out_shape=jax.ShapeDtypeStruct((T, LANES), _F32),
        grid=(T // tm,),
        in_specs=[pl.BlockSpec((tm, K), lambda i: (i, 0))],
        out_specs=pl.BlockSpec((tm, LANES), lambda i: (i, 0)),
        compiler_params=_params("parallel"),
        name="row_rstd",
    )(xb)


def _sigmoid(x):
    return 1.0 / (1.0 + jnp.exp2(x * (-LOG2E)))


def _gelu_tanh(x):
    inner = x * (GELU_C1 + GELU_C3 * (x * x))
    return (0.5 * x) * (1.0 + jnp.tanh(inner))


def _proj_kernel(x_ref, rs_ref, w_ref, o_ref):
    o_ref[...] = _scale_rows(jnp.dot(x_ref[...], w_ref[...], preferred_element_type=_F32), rs_ref[...])


def _proj_mm(xb, rs, w, *, tm, tn):
    T, K = xb.shape
    N = w.shape[1]
    tm, tn = min(tm, T), min(tn, N)
    return pl.pallas_call(
        _proj_kernel,
        out_shape=jax.ShapeDtypeStruct((T, N), _F32),
        grid=(T // tm, N // tn),
        in_specs=[pl.BlockSpec((tm, K), lambda i, j: (i, 0)),
                  pl.BlockSpec((tm, LANES), lambda i, j: (i, 0)),
                  pl.BlockSpec((K, tn), lambda i, j: (0, j))],
        out_specs=pl.BlockSpec((tm, tn), lambda i, j: (i, j)),
        compiler_params=_params("parallel", "parallel"),
        name="proj_mm",
    )(xb, rs, w)


def _res_kernel(y_ref, w_ref, r_ref, o_ref, ob_ref, rs_ref, *, n_cols):
    j = pl.program_id(1)

    @pl.when(j == 0)
    def _():
        rs_ref[...] = jnp.zeros_like(rs_ref)

    o = r_ref[...] + jnp.dot(y_ref[...], w_ref[...], preferred_element_type=_F32)
    o_ref[...] = o
    ob_ref[...] = o.astype(_BF16)
    o2 = o * o
    part = o2[:, 0:LANES]
    for c in range(1, o.shape[1] // LANES):
        part = part + o2[:, c * LANES:(c + 1) * LANES]
    rs_ref[...] += part

    @pl.when(j == pl.num_programs(1) - 1)
    def _():
        ssq = jnp.sum(rs_ref[...], axis=-1, keepdims=True)
        rs_ref[...] = jnp.broadcast_to(lax.rsqrt(ssq * (1.0 / n_cols) + NORM_EPS), rs_ref.shape)


def _res_mm(y, w, res, *, tm, tn):
    T, K = y.shape
    N = w.shape[1]
    tm, tn = min(tm, T), min(tn, N)
    return pl.pallas_call(
        functools.partial(_res_kernel, n_cols=N),
        out_shape=(jax.ShapeDtypeStruct((T, N), _F32), jax.ShapeDtypeStruct((T, N), _BF16),
                   jax.ShapeDtypeStruct((T, LANES), _F32)),
        grid=(T // tm, N // tn),
        in_specs=[pl.BlockSpec((tm, K), lambda i, j: (i, 0)),
                  pl.BlockSpec((K, tn), lambda i, j: (0, j)),
                  pl.BlockSpec((tm, tn), lambda i, j: (i, j))],
        out_specs=(pl.BlockSpec((tm, tn), lambda i, j: (i, j)),
                   pl.BlockSpec((tm, tn), lambda i, j: (i, j)),
                   pl.BlockSpec((tm, LANES), lambda i, j: (i, 0))),
        compiler_params=_params("parallel", "arbitrary"),
        name="res_mm",
    )(y, w, res)


def _glu_kernel(x_ref, rs_ref, wg_ref, wu_ref, o_ref):
    rs = rs_ref[...]
    g = _scale_rows(jnp.dot(x_ref[...], wg_ref[...], preferred_element_type=_F32), rs)
    u = _scale_rows(jnp.dot(x_ref[...], wu_ref[...], preferred_element_type=_F32), rs)
    o_ref[...] = (g * _sigmoid(g) * u).astype(o_ref.dtype)


def _glu_mm(xb, rs, wg, wu, *, tm, tn):
    T, K = xb.shape
    N = wg.shape[1]
    tm, tn = min(tm, T), min(tn, N)
    return pl.pallas_call(
        _glu_kernel,
        out_shape=jax.ShapeDtypeStruct((T, N), _BF16),
        grid=(T // tm, N // tn),
        in_specs=[pl.BlockSpec((tm, K), lambda i, j: (i, 0)),
                  pl.BlockSpec((tm, LANES), lambda i, j: (i, 0)),
                  pl.BlockSpec((K, tn), lambda i, j: (0, j)),
                  pl.BlockSpec((K, tn), lambda i, j: (0, j))],
        out_specs=pl.BlockSpec((tm, tn), lambda i, j: (i, j)),
        compiler_params=_params("parallel", "parallel"),
        name="glu_mm",
    )(xb, rs, wg, wu)


def _moe_glu_kernel(x_ref, rs_ref, wg_ref, wu_ref, gt_ref, o_ref):
    e = pl.program_id(1)
    rs = rs_ref[...]
    g = _scale_rows(jnp.dot(x_ref[...], wg_ref[...], preferred_element_type=_F32), rs)
    u = _scale_rows(jnp.dot(x_ref[...], wu_ref[...], preferred_element_type=_F32), rs)
    gates = gt_ref[...]
    lane = lax.broadcasted_iota(jnp.int32, gates.shape, 1)
    ge = jnp.sum(jnp.where(lane == e, gates, 0.0), axis=-1, keepdims=True)
    o_ref[...] = (g * _sigmoid(g) * u * ge).astype(o_ref.dtype)


def _moe_glu_mm(xb, rs, wg, wu, gates, *, tm):
    T, K = xb.shape
    E, _, F = wg.shape
    tm = min(tm, T)
    return pl.pallas_call(
        _moe_glu_kernel,
        out_shape=jax.ShapeDtypeStruct((T, E * F), _BF16),
        grid=(T // tm, E),
        in_specs=[pl.BlockSpec((tm, K), lambda i, e: (i, 0)),
                  pl.BlockSpec((tm, LANES), lambda i, e: (i, 0)),
                  pl.BlockSpec((None, K, F), lambda i, e: (e, 0, 0)),
                  pl.BlockSpec((None, K, F), lambda i, e: (e, 0, 0)),
                  pl.BlockSpec((tm, LANES), lambda i, e: (i, 0))],
        out_specs=pl.BlockSpec((tm, F), lambda i, e: (i, e)),
        compiler_params=_params("parallel", "parallel"),
        name="moe_glu_mm",
    )(xb, rs, wg, wu, gates)


def _rg_kernel(rec_ref, prev_ref, next_ref, gate_ref, cw_ref, cb_ref, wa_ref, ba_ref, wi_ref, bi_ref,
               cn_ref, y_ref, hs_ref, us_ref, ext_ref, a_ref, b_ref, hl_ref, pl_ref, carry_ref,
               *, ts, n_chunks, lc, sb):
    p = pl.program_id(2)
    k = pl.program_id(3)
    t = jnp.where(p == 0, k, n_chunks - 1 - k)
    base = pl.multiple_of(t * ts, ts)
    C = rec_ref.shape[-1]
    G = C // LANES
    R = ts // (SUBLANES * lc)
    L = sb + 2 * HALO
    per_sb = sb // lc
    row = lax.broadcasted_iota(jnp.int32, (SUBLANES, LANES), 0)
    streams = [(r, g) for r in range(R) for g in range(G)]

    @pl.when(k == 0)
    def _():
        carry_ref[...] = jnp.zeros_like(carry_ref)

    ba = ba_ref[...]
    bi = bi_ref[...]
    ch = cn_ref[...] * (0.5 * LOG2E)

    def gates(uh, s):
        ub = uh.astype(_BF16)
        ta = jnp.tanh(jnp.dot(ub, wa_ref[...], preferred_element_type=_F32) + ba)
        ti = jnp.tanh(jnp.dot(ub, wi_ref[...], preferred_element_type=_F32) + bi)
        a = jnp.exp2(ch + ch * ta)
        x = 1.0 - a * a
        b = x * lax.rsqrt(jnp.maximum(x, TINY)) * (1.0 + ti) * uh
        for i in range(per_sb):
            for g in range(G):
                a_ref[piece(g, s * per_sb + i)] = a[i * lc:(i + 1) * lc, g * LANES:(g + 1) * LANES]
                b_ref[piece(g, s * per_sb + i)] = b[i * lc:(i + 1) * lc, g * LANES:(g + 1) * LANES]

    def piece(g, q):
        r, j = q // SUBLANES, q % SUBLANES
        return (g, pl.ds(r * (lc * SUBLANES) + j, lc, stride=SUBLANES), slice(None))

    def tile(g, r, k8):
        return (g, pl.ds(k8 + r * (lc * SUBLANES), SUBLANES), slice(None))

    def local_scan(reverse):
        def body(i, hp):
            k8 = pl.multiple_of(((lc - 1 - i) if reverse else i) * SUBLANES, SUBLANES)
            out = []
            for n, (r, g) in enumerate(streams):
                at = a_ref[tile(g, r, k8)]
                h = at * hp[2 * n] + b_ref[tile(g, r, k8)]
                pr = at * hp[2 * n + 1]
                hl_ref[tile(g, r, k8)] = h
                pl_ref[tile(g, r, k8)] = pr
                out += [h, pr]
            return tuple(out)

        init = (jnp.zeros((SUBLANES, LANES), _F32), jnp.ones((SUBLANES, LANES), _F32)) * len(streams)
        return lax.fori_loop(0, lc, body, init, unroll=2)

    def sub_chunk_states(hp, reverse):
        cvecs = {}
        for g in range(G):
            cur = carry_ref[g]
            for r in (range(R - 1, -1, -1) if reverse else range(R)):
                n = streams.index((r, g))
                h_end, p_end = hp[2 * n], hp[2 * n + 1]
                cvec = jnp.zeros((SUBLANES, LANES), _F32)
                for j in (range(SUBLANES - 1, -1, -1) if reverse else range(SUBLANES)):
                    cvec = jnp.where(row == j, cur, cvec)
                    nxt = h_end + p_end * cur
                    cur = jnp.broadcast_to(nxt[j:j + 1, :], (SUBLANES, LANES))
                cvecs[(r, g)] = cvec
            carry_ref[g] = cur
        return cvecs

    @pl.when(p == 0)
    def _():
        ext_ref[pl.ds(HALO, ts), :] = rec_ref[...]
        ext_ref[pl.ds(0, HALO), :] = jnp.where(t > 0, prev_ref[...], 0.0)
        ext_ref[pl.ds(HALO + ts, HALO), :] = jnp.where(t < n_chunks - 1, next_ref[...], 0.0)
        cw = cw_ref[...]
        cb = cb_ref[...]

        def gates_body(s, _):
            r0 = pl.multiple_of(s * sb, sb)
            e = ext_ref[pl.ds(r0, L), :]
            uh = cb + cw[CONV_PAD_LEFT:CONV_PAD_LEFT + 1, :] * e[HALO:HALO + sb, :]
            for tap in range(CONV_WIDTH):
                d = tap - CONV_PAD_LEFT
                if d != 0:
                    uh = uh + cw[tap:tap + 1, :] * pltpu.roll(e, (-d) % L, 0)[HALO:HALO + sb, :]
            us_ref[pl.ds(base + r0, sb), :] = uh
            gates(uh, s)
            return 0

        lax.fori_loop(0, ts // sb, gates_body, 0, unroll=2)
        cvecs = sub_chunk_states(local_scan(False), False)

        def fix_body(kk, _):
            k8 = pl.multiple_of(kk * SUBLANES, SUBLANES)
            for r, g in streams:
                hs_ref[tile(g, r, base + k8)] = hl_ref[tile(g, r, k8)] + pl_ref[tile(g, r, k8)] * cvecs[(r, g)]
            return 0

        lax.fori_loop(0, lc, fix_body, 0, unroll=2)

    @pl.when(p == 1)
    def _():
        def gates_body(s, _):
            r0 = pl.multiple_of(s * sb, sb)
            gates(us_ref[pl.ds(base + r0, sb), :], s)
            return 0

        lax.fori_loop(0, ts // sb, gates_body, 0, unroll=4)
        cvecs = sub_chunk_states(local_scan(True), True)

        def fix_body(kk, _):
            k8 = pl.multiple_of(kk * SUBLANES, SUBLANES)
            for r, g in streams:
                h = hl_ref[tile(g, r, k8)] + pl_ref[tile(g, r, k8)] * cvecs[(r, g)]
                a_ref[tile(g, r, k8)] = h + hs_ref[tile(g, r, base + k8)]
            return 0

        lax.fori_loop(0, lc, fix_body, 0, unroll=2)

        def out_body(s, _):
            r0 = pl.multiple_of(s * sb, sb)
            gate = _gelu_tanh(gate_ref[pl.ds(r0, sb), :])
            for i in range(per_sb):
                for g in range(G):
                    cols = slice(g * LANES, (g + 1) * LANES)
                    y_ref[pl.ds(r0 + i * lc, lc), cols] = (
                        a_ref[piece(g, s * per_sb + i)] * gate[i * lc:(i + 1) * lc, cols]).astype(y_ref.dtype)
            return 0

        lax.fori_loop(0, ts // sb, out_body, 0)


def _rg_core(proj, conv_w, conv_b, w_a, b_a, w_i, b_i, cneg, *, ts, lc=SCAN_LC):
    B, S, D2 = proj.shape
    D = D2 // 2
    C = RG_BLOCK
    G = C // LANES
    nc = D // C
    ts = min(ts, S)
    n = S // ts
    lc = min(lc, ts // SUBLANES)
    sb = max(min(256, ts), lc)
    assert lc % SUBLANES == 0 and ts % (SUBLANES * lc) == 0 and sb % lc == 0 and ts % sb == 0
    tpb = ts // HALO

    def t_of(p, k):
        return jnp.where(p == 0, k, n - 1)

    def t_out(p, k):
        return jnp.where(p == 0, n - 1, n - 1 - k)

    kern = functools.partial(_rg_kernel, ts=ts, n_chunks=n, lc=lc, sb=sb)
    return pl.pallas_call(
        kern,
        out_shape=jax.ShapeDtypeStruct((B, S, D), _BF16),
        grid=(B, nc, 2, n),
        in_specs=[
            pl.BlockSpec((None, ts, C), lambda b, c, p, k: (b, t_of(p, k), nc + c)),
            pl.BlockSpec((None, HALO, C), lambda b, c, p, k: (b, jnp.maximum(t_of(p, k) * tpb - 1, 0), nc + c)),
            pl.BlockSpec((None, HALO, C),
                         lambda b, c, p, k: (b, jnp.minimum((t_of(p, k) + 1) * tpb, S // HALO - 1), nc + c)),
            pl.BlockSpec((None, ts, C), lambda b, c, p, k: (b, t_out(p, k), c)),
            pl.BlockSpec((CONV_WIDTH, C), lambda b, c, p, k: (0, c)),
            pl.BlockSpec((1, C), lambda b, c, p, k: (0, c)),
            pl.BlockSpec((None, None, C, C), lambda b, c, p, k: (p, c, 0, 0)),
            pl.BlockSpec((None, 1, C), lambda b, c, p, k: (p, 0, c)),
            pl.BlockSpec((None, None, C, C), lambda b, c, p, k: (p, c, 0, 0)),
            pl.BlockSpec((None, 1, C), lambda b, c, p, k: (p, 0, c)),
            pl.BlockSpec((None, 1, C), lambda b, c, p, k: (p, 0, c)),
        ],
        out_specs=pl.BlockSpec((None, ts, C), lambda b, c, p, k: (b, t_out(p, k), c)),
        scratch_shapes=[
            pltpu.VMEM((G, S, LANES), _F32),
            pltpu.VMEM((S, C), _F32),
            pltpu.VMEM((ts + 2 * HALO, C), _F32),
            pltpu.VMEM((G, ts, LANES), _F32),
            pltpu.VMEM((G, ts, LANES), _F32),
            pltpu.VMEM((G, ts, LANES), _F32),
            pltpu.VMEM((G, ts, LANES), _F32),
            pltpu.VMEM((G, SUBLANES, LANES), _F32),
        ],
        compiler_params=_params("parallel", "parallel", "arbitrary", "arbitrary"),
        name="rg_core",
    )(proj, proj, proj, proj, conv_w, conv_b, w_a, b_a, w_i, b_i, cneg)


def _pool_kernel(x_ref, prev_ref, next_ref, pw_ref, ps_ref, rh_ref, rl_ref,
                 o_ref, ob_ref, gt_ref, rs_ref, ext_ref, *, ts, n_chunks, seq, n_experts):
    t = pl.program_id(1)
    D = x_ref.shape[-1]
    G = len(POOL_WINDOWS)
    gs = D // G
    L = ts + 2 * HALO
    assert POOL_WINDOWS[-1] == 2 * HALO and all(w <= HALO for w in POOL_WINDOWS[:-1])

    def scaled(v):
        return v * lax.rsqrt(jnp.mean(v * v, axis=-1, keepdims=True) + NORM_EPS)

    ext_ref[pl.ds(HALO, ts), :] = scaled(x_ref[...])
    ext_ref[pl.ds(0, HALO), :] = jnp.where(t > 0, scaled(prev_ref[...]), 0.0)
    ext_ref[pl.ds(HALO + ts, HALO), :] = jnp.where(t < n_chunks - 1, scaled(next_ref[...]), 0.0)

    def ahead(v, n):
        return pltpu.roll(v, L - n, 0)

    tpos = t * ts + lax.broadcasted_iota(jnp.int32, (ts, 1), 0)
    ssq = jnp.zeros((ts, 1), _F32)
    logits = jnp.zeros((ts, LANES), _F32)
    for gi, win in enumerate(POOL_WINDOWS):
        cols = pl.ds(gi * gs, gs)
        e = ext_ref[:, cols]
        if win == 2:
            s = (pltpu.roll(e, 1, 0) + e)[HALO:HALO + ts, :]
        else:
            f, w = e, 1
            while w < min(win, HALO):
                f = f + ahead(f, w)
                w *= 2
            if win < 2 * HALO:
                s = pltpu.roll(f, win // 2, 0)[HALO:HALO + ts, :]
            else:
                s = f[0:ts, :] + f[HALO:HALO + ts, :]
        half = win // 2
        cnt = jnp.minimum(tpos + (win - half), seq) - jnp.maximum(tpos - half, 0)
        d = s * (1.0 / cnt.astype(_F32)) - e[HALO:HALO + ts, :]
        y = jnp.dot(d.astype(_BF16), pw_ref[gi], preferred_element_type=_F32)
        o = x_ref[:, cols] + y * ps_ref[:, cols]
        o_ref[:, cols] = o
        oh = o.astype(_BF16)
        ob_ref[:, cols] = oh
        ssq = ssq + jnp.sum(o * o, axis=-1, keepdims=True)
        ol = (o - oh.astype(_F32)).astype(_BF16)
        rh = rh_ref[cols, :]
        logits = (logits + jnp.dot(oh, rh, preferred_element_type=_F32)
                  + jnp.dot(oh, rl_ref[cols, :], preferred_element_type=_F32)
                  + jnp.dot(ol, rh, preferred_element_type=_F32))

    rs = lax.rsqrt(ssq * (1.0 / D) + NORM_EPS)
    rs_ref[...] = jnp.broadcast_to(rs, rs_ref.shape)
    logits = logits * rs
    lane = lax.broadcasted_iota(jnp.int32, (ts, LANES), 1).astype(_F32)
    neg = jnp.float32(-jnp.inf)
    logits = jnp.where(lane < n_experts, logits, neg)
    m1 = jnp.max(logits, axis=-1, keepdims=True)
    i1 = jnp.min(jnp.where(logits == m1, lane, float(LANES)), axis=-1, keepdims=True)
    rest = jnp.where(lane == i1, neg, logits)
    m2 = jnp.max(rest, axis=-1, keepdims=True)
    i2 = jnp.min(jnp.where(rest == m2, lane, float(LANES)), axis=-1, keepdims=True)
    e2 = jnp.exp(m2 - m1)
    g1 = 1.0 / (1.0 + e2)
    gt_ref[...] = jnp.where(lane == i1, g1, 0.0) + jnp.where(lane == i2, e2 * g1, 0.0)


def _pool_mix(x, pool_w, pool_scale, r_hi, r_lo, n_experts, *, ts):
    B, S, D = x.shape
    ts = min(ts, S)
    n = S // ts
    tpb = ts // HALO
    G, gs, _ = pool_w.shape
    kern = functools.partial(_pool_kernel, ts=ts, n_chunks=n, seq=S, n_experts=n_experts)
    return pl.pallas_call(
        kern,
        out_shape=(jax.ShapeDtypeStruct((B, S, D), _F32), jax.ShapeDtypeStruct((B, S, D), _BF16),
                   jax.ShapeDtypeStruct((B, S, LANES), _F32), jax.ShapeDtypeStruct((B, S, LANES), _F32)),
        grid=(B, n),
        in_specs=[
            pl.BlockSpec((None, ts, D), lambda b, t: (b, t, 0)),
            pl.BlockSpec((None, HALO, D), lambda b, t: (b, jnp.maximum(t * tpb - 1, 0), 0)),
            pl.BlockSpec((None, HALO, D), lambda b, t: (b, jnp.minimum((t + 1) * tpb, S // HALO - 1), 0)),
            pl.BlockSpec((G, gs, gs), lambda b, t: (0, 0, 0)),
            pl.BlockSpec((1, D), lambda b, t: (0, 0)),
            pl.BlockSpec((D, LANES), lambda b, t: (0, 0)),
            pl.BlockSpec((D, LANES), lambda b, t: (0, 0)),
        ],
        out_specs=(pl.BlockSpec((None, ts, D), lambda b, t: (b, t, 0)),
                   pl.BlockSpec((None, ts, D), lambda b, t: (b, t, 0)),
                   pl.BlockSpec((None, ts, LANES), lambda b, t: (b, t, 0)),
                   pl.BlockSpec((None, ts, LANES), lambda b, t: (b, t, 0))),
        scratch_shapes=[pltpu.VMEM((ts + 2 * HALO, D), _F32)],
        compiler_params=_params("parallel", "parallel"),
        name="pool_mix",
    )(x, x, x, pool_w, pool_scale, r_hi, r_lo)


def _res_norm_kernel(y_ref, w_ref, r_ref, g_ref, o_ref, *, tn, rb):
    j = pl.program_id(1)
    c0 = pl.multiple_of(j * tn, tn)
    o_ref[:, pl.ds(c0, tn)] = r_ref[...] + jnp.dot(y_ref[...], w_ref[...], preferred_element_type=_F32)

    @pl.when(j == pl.num_programs(1) - 1)
    def _():
        def body(s, _):
            r0 = pl.multiple_of(s * rb, rb)
            x = o_ref[pl.ds(r0, rb), :]
            o_ref[pl.ds(r0, rb), :] = (x * lax.rsqrt(jnp.mean(x * x, axis=-1, keepdims=True) + NORM_EPS)
                                       * g_ref[...])
            return 0

        lax.fori_loop(0, o_ref.shape[0] // rb, body, 0)


def _res_norm_mm(y, w, res, gain, *, tm, tn):
    T, K = y.shape
    N = w.shape[1]
    tm, tn = min(tm, T), min(tn, N)
    rb = min(64, tm)
    return pl.pallas_call(
        functools.partial(_res_norm_kernel, tn=tn, rb=rb),
        out_shape=jax.ShapeDtypeStruct((T, N), _F32),
        grid=(T // tm, N // tn),
        in_specs=[pl.BlockSpec((tm, K), lambda i, j: (i, 0)),
                  pl.BlockSpec((K, tn), lambda i, j: (0, j)),
                  pl.BlockSpec((tm, tn), lambda i, j: (i, j)),
                  pl.BlockSpec((1, N), lambda i, j: (0, 0))],
        out_specs=pl.BlockSpec((tm, N), lambda i, j: (i, 0)),
        compiler_params=_params("parallel", "arbitrary"),
        name="res_norm_mm",
    )(y, w, res, gain)


def _fold(gain, w):
    return (gain[:, None] * w).astype(_BF16)


def _prep_weights(mix_norm, ffn_norm, final_norm, rg_w_in, rg_conv_w, rg_conv_b, rg_w_a, rg_b_a, rg_w_i,
                  rg_b_i, rg_lambda, rg_w_out, pool_w, pool_scale, ffn_w_gate, ffn_w_up, ffn_w_down,
                  moe_router, moe_w_gate, moe_w_up, moe_w_down):
    depth, D = mix_norm.shape
    layers = []
    for i in range(depth):
        j = i // 2
        lw = {}
        if i % 2 == 0:
            lw["w_in"] = _fold(mix_norm[i], rg_w_in[j])
            lw["conv_w"] = 0.5 * rg_conv_w[j]
            lw["conv_b"] = 0.5 * rg_conv_b[j][None, :]
            lw["w_a"] = rg_w_a[j].astype(_BF16)
            lw["b_a"] = 0.5 * rg_b_a[j][:, None, :]
            lw["w_i"] = rg_w_i[j].astype(_BF16)
            lw["b_i"] = 0.5 * rg_b_i[j][:, None, :]
            lw["cneg"] = (-RG_C * jax.nn.softplus(-rg_lambda[j]))[:, None, :]
            lw["w_out"] = rg_w_out[j].astype(_BF16)
            lw["w_gate"] = _fold(ffn_norm[i], ffn_w_gate[j])
            lw["w_up"] = _fold(ffn_norm[i], ffn_w_up[j])
            lw["w_down"] = ffn_w_down[j].astype(_BF16)
        else:
            E = moe_router.shape[-1]
            G, gs = pool_w.shape[1:3]
            lw["pool_w"] = (mix_norm[i].reshape(G, gs, 1) * pool_w[j]).astype(_BF16)
            lw["pool_scale"] = pool_scale[j][None, :]
            router = jnp.pad(ffn_norm[i][:, None] * moe_router[j], ((0, 0), (0, LANES - E)))
            lw["r_hi"] = router.astype(_BF16)
            lw["r_lo"] = (router - lw["r_hi"].astype(_F32)).astype(_BF16)
            lw["n_experts"] = E
            lw["w_gate"] = _fold(ffn_norm[i], moe_w_gate[j])
            lw["w_up"] = _fold(ffn_norm[i], moe_w_up[j])
            lw["w_down"] = moe_w_down[j].reshape(-1, D).astype(_BF16)
        layers.append(lw)
    return layers, final_norm[None, :]


def _trunk(x, layers, final_gain, *, tm, tn, rg_ts, pool_ts):
    B, S, D = x.shape
    T = B * S
    xf = x.reshape(T, D)
    xb = xf.astype(_BF16)
    rs = _row_rstd(xb, tm=tm)
    for i, lw in enumerate(layers):
        if i % 2 == 0:
            proj = _proj_mm(xb, rs, lw["w_in"], tm=tm, tn=2 * tn)
            y = _rg_core(proj.reshape(B, S, 2 * D), lw["conv_w"], lw["conv_b"], lw["w_a"], lw["b_a"],
                         lw["w_i"], lw["b_i"], lw["cneg"], ts=rg_ts)
            xf, xb, rs = _res_mm(y.reshape(T, D), lw["w_out"], xf, tm=tm, tn=tn)
            h = _glu_mm(xb, rs, lw["w_gate"], lw["w_up"], tm=tm, tn=tn)
        else:
            o, ob, gates, rs = _pool_mix(xf.reshape(B, S, D), lw["pool_w"], lw["pool_scale"],
                                         lw["r_hi"], lw["r_lo"], lw["n_experts"], ts=pool_ts)
            xf, xb, rs = o.reshape(T, D), ob.reshape(T, D), rs.reshape(T, LANES)
            h = _moe_glu_mm(xb, rs, lw["w_gate"], lw["w_up"], gates.reshape(T, LANES), tm=tm)
        if i + 1 < len(layers):
            xf, xb, rs = _res_mm(h, lw["w_down"], xf, tm=tm, tn=tn)
    return _res_norm_mm(h, lw["w_down"], xf, final_gain, tm=tm // 2, tn=tn).reshape(B, S, D)


def kernel(x_prompt, x_sample, mix_norm, ffn_norm, final_norm, rg_w_in, rg_conv_w, rg_conv_b, rg_w_a, rg_b_a, rg_w_i, rg_b_i, rg_lambda, rg_w_out, pool_w, pool_scale, ffn_w_gate, ffn_w_up, ffn_w_down, moe_router, moe_w_gate, moe_w_up, moe_w_down):
    layers, final_gain = _prep_weights(
        mix_norm, ffn_norm, final_norm, rg_w_in, rg_conv_w, rg_conv_b, rg_w_a, rg_b_a, rg_w_i, rg_b_i,
        rg_lambda, rg_w_out, pool_w, pool_scale, ffn_w_gate, ffn_w_up, ffn_w_down, moe_router,
        moe_w_gate, moe_w_up, moe_w_down)
    y_prompt = _trunk(x_prompt, layers, final_gain, **_tiles(x_prompt.shape[1]))
    y_sample = _trunk(x_sample, layers, final_gain, **_tiles(x_sample.shape[1]))
    return (y_prompt, y_sample)
```

```python
import functools
import math

import jax
import jax.numpy as jnp
from jax import lax
from jax.experimental import pallas as pl
from jax.experimental.pallas import tpu as pltpu

NORM_EPS = 1e-6
RG_C = 8.0
RG_BLOCK = 256
CONV_WIDTH = 4
CONV_PAD_LEFT = (CONV_WIDTH - 1) // 2
POOL_WINDOWS = (2, 4, 8, 16)
TOP_K = 2
SUBLANES = 8
LANES = 128
HALO = 8
VMEM_LIMIT = 56 * 1024 * 1024
LOG2E = 1.4426950408889634
GELU_C1 = math.sqrt(2.0 / math.pi)
GELU_C3 = 0.044715 * GELU_C1
TINY = 1e-30
SCAN_LC = 64

_BF16 = jnp.bfloat16
_F32 = jnp.float32


def _tiles(seq):
    return dict(tm=1024, tn=512, rg_ts=2048 if seq <= 2048 else 1024, pool_ts=256)


def _params(*sem):
    return pltpu.CompilerParams(dimension_semantics=sem, vmem_limit_bytes=VMEM_LIMIT)


def _cast_rstd_kernel(x_ref, xb_ref, rs_ref):
    x = x_ref[...]
    xb_ref[...] = x.astype(_BF16)
    rs = lax.rsqrt(jnp.mean(x * x, axis=-1, keepdims=True) + NORM_EPS)
    rs_ref[...] = jnp.broadcast_to(rs, rs_ref.shape)


def _scale_rows(v, rs):
    return v * pltpu.repeat(rs, v.shape[1] // LANES, axis=1)


def _cast_rstd(x, *, tm):
    T, K = x.shape
    tm = min(tm, T)
    return pl.pallas_call(
        _cast_rstd_kernel,
        out_shape=(jax.ShapeDtypeStruct((T, K), _BF16), jax.ShapeDtypeStruct((T, LANES), _F32)),
        grid=(T // tm,),
        in_specs=[pl.BlockSpec((tm, K), lambda i: (i, 0))],
        out_specs=(pl.BlockSpec((tm, K), lambda i: (i, 0)), pl.BlockSpec((tm, LANES), lambda i: (i, 0))),
        compiler_params=_params("parallel"),
        name="cast_rstd",
    )(x)


def _sigmoid(x):
    return 1.0 / (1.0 + jnp.exp2(x * (-LOG2E)))


def _gelu_tanh(x):
    inner = x * (GELU_C1 + GELU_C3 * (x * x))
    return (0.5 * x) * (1.0 + jnp.tanh(inner))


def _proj_kernel(x_ref, rs_ref, w_ref, o_ref):
    o_ref[...] = _scale_rows(jnp.dot(x_ref[...], w_ref[...], preferred_element_type=_F32), rs_ref[...])


def _proj_mm(xb, rs, w, *, tm, tn):
    T, K = xb.shape
    N = w.shape[1]
    tm, tn = min(tm, T), min(tn, N)
    return pl.pallas_call(
        _proj_kernel,
        out_shape=jax.ShapeDtypeStruct((T, N), _F32),
        grid=(T // tm, N // tn),
        in_specs=[pl.BlockSpec((tm, K), lambda i, j: (i, 0)),
                  pl.BlockSpec((tm, LANES), lambda i, j: (i, 0)),
                  pl.BlockSpec((K, tn), lambda i, j: (0, j))],
        out_specs=pl.BlockSpec((tm, tn), lambda i, j: (i, j)),
        compiler_params=_params("parallel", "parallel"),
        name="proj_mm",
    )(xb, rs, w)


def _res_kernel(y_ref, w_ref, r_ref, o_ref, ob_ref, rs_ref, *, n_cols):
    j = pl.program_id(1)

    @pl.when(j == 0)
    def _():
        rs_ref[...] = jnp.zeros_like(rs_ref)

    o = r_ref[...] + jnp.dot(y_ref[...], w_ref[...], preferred_element_type=_F32)
    o_ref[...] = o
    ob_ref[...] = o.astype(_BF16)
    o2 = o * o
    part = o2[:, 0:LANES]
    for c in range(1, o.shape[1] // LANES):
        part = part + o2[:, c * LANES:(c + 1) * LANES]
    rs_ref[...] += part

    @pl.when(j == pl.num_programs(1) - 1)
    def _():
        ssq = jnp.sum(rs_ref[...], axis=-1, keepdims=True)
        rs_ref[...] = jnp.broadcast_to(lax.rsqrt(ssq * (1.0 / n_cols) + NORM_EPS), rs_ref.shape)


def _res_mm(y, w, res, *, tm, tn):
    T, K = y.shape
    N = w.shape[1]
    tm, tn = min(tm, T), min(tn, N)
    return pl.pallas_call(
        functools.partial(_res_kernel, n_cols=N),
        out_shape=(jax.ShapeDtypeStruct((T, N), _F32), jax.ShapeDtypeStruct((T, N), _BF16),
                   jax.ShapeDtypeStruct((T, LANES), _F32)),
        grid=(T // tm, N // tn),
        in_specs=[pl.BlockSpec((tm, K), lambda i, j: (i, 0)),
                  pl.BlockSpec((K, tn), lambda i, j: (0, j)),
                  pl.BlockSpec((tm, tn), lambda i, j: (i, j))],
        out_specs=(pl.BlockSpec((tm, tn), lambda i, j: (i, j)),
                   pl.BlockSpec((tm, tn), lambda i, j: (i, j)),
                   pl.BlockSpec((tm, LANES), lambda i, j: (i, 0))),
        compiler_params=_params("parallel", "arbitrary"),
        name="res_mm",
    )(y, w, res)


def _glu_kernel(x_ref, rs_ref, wg_ref, wu_ref, o_ref):
    rs = rs_ref[...]
    g = _scale_rows(jnp.dot(x_ref[...], wg_ref[...], preferred_element_type=_F32), rs)
    u = _scale_rows(jnp.dot(x_ref[...], wu_ref[...], preferred_element_type=_F32), rs)
    o_ref[...] = (g * _sigmoid(g) * u).astype(o_ref.dtype)


def _glu_mm(xb, rs, wg, wu, *, tm, tn):
    T, K = xb.shape
    N = wg.shape[1]
    tm, tn = min(tm, T), min(tn, N)
    return pl.pallas_call(
        _glu_kernel,
        out_shape=jax.ShapeDtypeStruct((T, N), _BF16),
        grid=(T // tm, N // tn),
        in_specs=[pl.BlockSpec((tm, K), lambda i, j: (i, 0)),
                  pl.BlockSpec((tm, LANES), lambda i, j: (i, 0)),
                  pl.BlockSpec((K, tn), lambda i, j: (0, j)),
                  pl.BlockSpec((K, tn), lambda i, j: (0, j))],
        out_specs=pl.BlockSpec((tm, tn), lambda i, j: (i, j)),
        compiler_params=_params("parallel", "parallel"),
        name="glu_mm",
    )(xb, rs, wg, wu)


def _moe_glu_kernel(x_ref, rs_ref, wg_ref, wu_ref, gt_ref, o_ref):
    e = pl.program_id(1)
    rs = rs_ref[...]
    g = _scale_rows(jnp.dot(x_ref[...], wg_ref[...], preferred_element_type=_F32), rs)
    u = _scale_rows(jnp.dot(x_ref[...], wu_ref[...], preferred_element_type=_F32), rs)
    gates = gt_ref[...]
    lane = lax.broadcasted_iota(jnp.int32, gates.shape, 1)
    ge = jnp.sum(jnp.where(lane == e, gates, 0.0), axis=-1, keepdims=True)
    o_ref[...] = (g * _sigmoid(g) * u * ge).astype(o_ref.dtype)


def _moe_glu_mm(xb, rs, wg, wu, gates, *, tm):
    T, K = xb.shape
    E, _, F = wg.shape
    tm = min(tm, T)
    return pl.pallas_call(
        _moe_glu_kernel,
        out_shape=jax.ShapeDtypeStruct((T, E * F), _BF16),
        grid=(T // tm, E),
        in_specs=[pl.BlockSpec((tm, K), lambda i, e: (i, 0)),
                  pl.BlockSpec((tm, LANES), lambda i, e: (i, 0)),
                  pl.BlockSpec((None, K, F), lambda i, e: (e, 0, 0)),
                  pl.BlockSpec((None, K, F), lambda i, e: (e, 0, 0)),
                  pl.BlockSpec((tm, LANES), lambda i, e: (i, 0))],
        out_specs=pl.BlockSpec((tm, F), lambda i, e: (i, e)),
        compiler_params=_params("parallel", "parallel"),
        name="moe_glu_mm",
    )(xb, rs, wg, wu, gates)


def _rg_kernel(rec_ref, prev_ref, next_ref, gate_ref, cw_ref, cb_ref, wa_ref, ba_ref, wi_ref, bi_ref,
               cn_ref, y_ref, hs_ref, us_ref, xi_ref, a_ref, b_ref, hl_ref, pl_ref, carry_ref,
               *, ts, n_chunks, lc, sb):
    p = pl.program_id(2)
    k = pl.program_id(3)
    t = jnp.where(p == 0, k, n_chunks - 1 - k)
    base = pl.multiple_of(t * ts, ts)
    C = rec_ref.shape[-1]
    G = C // LANES
    R = ts // (SUBLANES * lc)
    me = lc + CONV_WIDTH - 1
    nk = sb // SUBLANES
    per_stream = lc // nk
    row = lax.broadcasted_iota(jnp.int32, (SUBLANES, LANES), 0)
    streams = [(r, g) for r in range(R) for g in range(G)]

    @pl.when(k == 0)
    def _():
        carry_ref[...] = jnp.zeros_like(carry_ref)

    ba = ba_ref[...]
    bi = bi_ref[...]
    ch = cn_ref[...] * (0.5 * LOG2E)

    def lanes(g):
        return slice(g * LANES, (g + 1) * LANES)

    def gates(uh, r0):
        ub = uh.astype(_BF16)
        ta = jnp.tanh(jnp.dot(ub, wa_ref[...], preferred_element_type=_F32) + ba)
        ti = jnp.tanh(jnp.dot(ub, wi_ref[...], preferred_element_type=_F32) + bi)
        a = jnp.exp2(ch + ch * ta)
        x = 1.0 - a * a
        b = x * lax.rsqrt(jnp.maximum(x, TINY)) * (1.0 + ti) * uh
        for g in range(G):
            a_ref[g, pl.ds(r0, sb), :] = a[:, lanes(g)]
            b_ref[g, pl.ds(r0, sb), :] = b[:, lanes(g)]

    def piece(g, q):
        r, j = q // SUBLANES, q % SUBLANES
        return (g, pl.ds(r * (lc * SUBLANES) + j, lc, stride=SUBLANES), slice(None))

    def tile(g, r, k8):
        return (g, pl.ds(k8 + r * (lc * SUBLANES), SUBLANES), slice(None))

    def local_scan(reverse):
        def body(i, hp):
            k8 = pl.multiple_of(((lc - 1 - i) if reverse else i) * SUBLANES, SUBLANES)
            out = []
            for n, (r, g) in enumerate(streams):
                at = a_ref[tile(g, r, k8)]
                h = at * hp[2 * n] + b_ref[tile(g, r, k8)]
                pr = at * hp[2 * n + 1]
                hl_ref[tile(g, r, k8)] = h
                pl_ref[tile(g, r, k8)] = pr
                out += [h, pr]
            return tuple(out)

        init = (jnp.zeros((SUBLANES, LANES), _F32), jnp.ones((SUBLANES, LANES), _F32)) * len(streams)
        return lax.fori_loop(0, lc, body, init, unroll=2)

    def sub_chunk_states(hp, reverse):
        cvecs = {}
        edge = 0 if reverse else SUBLANES - 1
        for g in range(G):
            cur = carry_ref[g]
            for r in (range(R - 1, -1, -1) if reverse else range(R)):
                n = streams.index((r, g))
                b, a = hp[2 * n], hp[2 * n + 1]
                for d in (1, 2, 4):
                    shift, valid = (SUBLANES - d, row < SUBLANES - d) if reverse else (d, row >= d)
                    b = b + a * jnp.where(valid, pltpu.roll(b, shift, 0), 0.0)
                    a = a * jnp.where(valid, pltpu.roll(a, shift, 0), 1.0)
                after = b + a * cur
                first = row == (SUBLANES - 1 if reverse else 0)
                cvecs[(r, g)] = jnp.where(first, cur, pltpu.roll(after, SUBLANES - 1 if reverse else 1, 0))
                cur = jnp.broadcast_to(after[edge:edge + 1, :], (SUBLANES, LANES))
            carry_ref[g] = cur
        return cvecs

    def sub_block(s):
        r, h = s // per_stream, s % per_stream
        return (pl.multiple_of((r * lc + h * nk) * SUBLANES, SUBLANES),
                pl.multiple_of((r * me + h * nk) * SUBLANES, SUBLANES))

    @pl.when(p == 0)
    def _():
        for q in range(SUBLANES * R):
            r, j = divmod(q, SUBLANES)
            for g in range(G):
                xi_ref[g, pl.ds((r * me + 1) * SUBLANES + j, lc, stride=SUBLANES), :] = (
                    rec_ref[pl.ds(q * lc, lc), lanes(g)])

        def step_tile(g, r, kk):
            return xi_ref[g, pl.ds((r * me + 1 + kk) * SUBLANES, SUBLANES), :]

        def bcast(v):
            return jnp.broadcast_to(v, (SUBLANES, LANES))

        for g in range(G):
            before = jnp.where(t > 0, prev_ref[HALO - 1:HALO, lanes(g)], 0.0)
            after = jnp.where(t < n_chunks - 1, next_ref[0:2, lanes(g)], 0.0)
            for r in range(R):
                fill = step_tile(g, r - 1, lc - 1)[SUBLANES - 1:SUBLANES, :] if r > 0 else before
                xi_ref[g, pl.ds(r * me * SUBLANES, SUBLANES), :] = jnp.where(
                    row == 0, bcast(fill), pltpu.roll(step_tile(g, r, lc - 1), 1, 0))
                for d in range(2):
                    fill = step_tile(g, r + 1, d)[0:1, :] if r + 1 < R else after[d:d + 1, :]
                    xi_ref[g, pl.ds((r * me + 1 + lc + d) * SUBLANES, SUBLANES), :] = jnp.where(
                        row == SUBLANES - 1, bcast(fill), pltpu.roll(step_tile(g, r, d), SUBLANES - 1, 0))

        cw = cw_ref[...]
        cb = cb_ref[...]

        def gates_body(s, _):
            r0, x0 = sub_block(s)
            parts = []
            for g in range(G):
                win = xi_ref[g, pl.ds(x0, sb + (CONV_WIDTH - 1) * SUBLANES), :]
                uh = cb[:, lanes(g)]
                for tap in range(CONV_WIDTH):
                    uh = uh + cw[tap:tap + 1, lanes(g)] * win[tap * SUBLANES:tap * SUBLANES + sb, :]
                us_ref[g, pl.ds(base + r0, sb), :] = uh
                parts.append(uh)
            gates(jnp.concatenate(parts, axis=1), r0)
            return 0

        lax.fori_loop(0, ts // sb, gates_body, 0, unroll=2)
        cvecs = sub_chunk_states(local_scan(False), False)

        def fix_body(kk, _):
            k8 = pl.multiple_of(kk * SUBLANES, SUBLANES)
            for r, g in streams:
                hs_ref[tile(g, r, base + k8)] = hl_ref[tile(g, r, k8)] + pl_ref[tile(g, r, k8)] * cvecs[(r, g)]
            return 0

        lax.fori_loop(0, lc, fix_body, 0, unroll=2)

    @pl.when(p == 1)
    def _():
        def gates_body(s, _):
            r0, _ = sub_block(s)
            gates(jnp.concatenate([us_ref[g, pl.ds(base + r0, sb), :] for g in range(G)], axis=1), r0)
            return 0

        lax.fori_loop(0, ts // sb, gates_body, 0, unroll=4)
        cvecs = sub_chunk_states(local_scan(True), True)

        def fix_body(kk, _):
            k8 = pl.multiple_of(kk * SUBLANES, SUBLANES)
            for r, g in streams:
                h = hl_ref[tile(g, r, k8)] + pl_ref[tile(g, r, k8)] * cvecs[(r, g)]
                a_ref[tile(g, r, k8)] = h + hs_ref[tile(g, r, base + k8)]
            return 0

        lax.fori_loop(0, lc, fix_body, 0, unroll=2)

        def out_body(s, _):
            r0 = pl.multiple_of(s * sb, sb)
            gate = _gelu_tanh(gate_ref[pl.ds(r0, sb), :])
            for i in range(sb // lc):
                for g in range(G):
                    y_ref[pl.ds(r0 + i * lc, lc), lanes(g)] = (
                        a_ref[piece(g, s * (sb // lc) + i)] * gate[i * lc:(i + 1) * lc, lanes(g)]
                    ).astype(y_ref.dtype)
            return 0

        lax.fori_loop(0, ts // sb, out_body, 0)


def _rg_core(proj, conv_w, conv_b, w_a, b_a, w_i, b_i, cneg, *, ts, lc=SCAN_LC):
    B, S, D2 = proj.shape
    D = D2 // 2
    C = RG_BLOCK
    G = C // LANES
    nc = D // C
    ts = min(ts, S)
    n = S // ts
    lc = min(lc, ts // SUBLANES)
    sb = min(256, lc * SUBLANES)
    assert lc % SUBLANES == 0 and ts % (SUBLANES * lc) == 0 and (lc * SUBLANES) % sb == 0 and sb % lc == 0
    assert (CONV_WIDTH, CONV_PAD_LEFT) == (4, 1)
    tpb = ts // HALO

    def t_of(p, k):
        return jnp.where(p == 0, k, n - 1)

    def t_out(p, k):
        return jnp.where(p == 0, n - 1, n - 1 - k)

    kern = functools.partial(_rg_kernel, ts=ts, n_chunks=n, lc=lc, sb=sb)
    return pl.pallas_call(
        kern,
        out_shape=jax.ShapeDtypeStruct((B, S, D), _BF16),
        grid=(B, nc, 2, n),
        in_specs=[
            pl.BlockSpec((None, ts, C), lambda b, c, p, k: (b, t_of(p, k), nc + c)),
            pl.BlockSpec((None, HALO, C), lambda b, c, p, k: (b, jnp.maximum(t_of(p, k) * tpb - 1, 0), nc + c)),
            pl.BlockSpec((None, HALO, C),
                         lambda b, c, p, k: (b, jnp.minimum((t_of(p, k) + 1) * tpb, S // HALO - 1), nc + c)),
            pl.BlockSpec((None, ts, C), lambda b, c, p, k: (b, t_out(p, k), c)),
            pl.BlockSpec((CONV_WIDTH, C), lambda b, c, p, k: (0, c)),
            pl.BlockSpec((1, C), lambda b, c, p, k: (0, c)),
            pl.BlockSpec((None, None, C, C), lambda b, c, p, k: (p, c, 0, 0)),
            pl.BlockSpec((None, 1, C), lambda b, c, p, k: (p, 0, c)),
            pl.BlockSpec((None, None, C, C), lambda b, c, p, k: (p, c, 0, 0)),
            pl.BlockSpec((None, 1, C), lambda b, c, p, k: (p, 0, c)),
            pl.BlockSpec((None, 1, C), lambda b, c, p, k: (p, 0, c)),
        ],
        out_specs=pl.BlockSpec((None, ts, C), lambda b, c, p, k: (b, t_out(p, k), c)),
        scratch_shapes=[
            pltpu.VMEM((G, S, LANES), _F32),
            pltpu.VMEM((G, S, LANES), _F32),
            pltpu.VMEM((G, (ts // lc // SUBLANES) * (lc + CONV_WIDTH - 1) * SUBLANES, LANES), _F32),
            pltpu.VMEM((G, ts, LANES), _F32),
            pltpu.VMEM((G, ts, LANES), _F32),
            pltpu.VMEM((G, ts, LANES), _F32),
            pltpu.VMEM((G, ts, LANES), _F32),
            pltpu.VMEM((G, SUBLANES, LANES), _F32),
        ],
        compiler_params=_params("parallel", "parallel", "arbitrary", "arbitrary"),
        name="rg_core",
    )(proj, proj, proj, proj, conv_w, conv_b, w_a, b_a, w_i, b_i, cneg)


def _pool_kernel(x_ref, prev_ref, next_ref, pw_ref, ps_ref, rh_ref, rl_ref,
                 o_ref, ob_ref, gt_ref, rs_ref, ext_ref, *, ts, n_chunks, seq, n_experts):
    t = pl.program_id(1)
    D = x_ref.shape[-1]
    G = len(POOL_WINDOWS)
    gs = D // G
    L = ts + 2 * HALO
    assert POOL_WINDOWS[-1] == 2 * HALO and all(w <= HALO for w in POOL_WINDOWS[:-1])

    def scaled(v):
        return v * lax.rsqrt(jnp.mean(v * v, axis=-1, keepdims=True) + NORM_EPS)

    ext_ref[pl.ds(HALO, ts), :] = scaled(x_ref[...])
    ext_ref[pl.ds(0, HALO), :] = jnp.where(t > 0, scaled(prev_ref[...]), 0.0)
    ext_ref[pl.ds(HALO + ts, HALO), :] = jnp.where(t < n_chunks - 1, scaled(next_ref[...]), 0.0)

    def ahead(v, n):
        return pltpu.roll(v, L - n, 0)

    tpos = t * ts + lax.broadcasted_iota(jnp.int32, (ts, 1), 0)
    ssq = jnp.zeros((ts, 1), _F32)
    logits = jnp.zeros((ts, LANES), _F32)
    for gi, win in enumerate(POOL_WINDOWS):
        cols = pl.ds(gi * gs, gs)
        e = ext_ref[:, cols]
        if win == 2:
            s = (pltpu.roll(e, 1, 0) + e)[HALO:HALO + ts, :]
        else:
            f, w = e, 1
            while w < min(win, HALO):
                f = f + ahead(f, w)
                w *= 2
            if win < 2 * HALO:
                s = pltpu.roll(f, win // 2, 0)[HALO:HALO + ts, :]
            else:
                s = f[0:ts, :] + f[HALO:HALO + ts, :]
        half = win // 2
        cnt = jnp.minimum(tpos + (win - half), seq) - jnp.maximum(tpos - half, 0)
        d = s * (1.0 / cnt.astype(_F32)) - e[HALO:HALO + ts, :]
        y = jnp.dot(d.astype(_BF16), pw_ref[gi], preferred_element_type=_F32)
        o = x_ref[:, cols] + y * ps_ref[:, cols]
        o_ref[:, cols] = o
        oh = o.astype(_BF16)
        ob_ref[:, cols] = oh
        ssq = ssq + jnp.sum(o * o, axis=-1, keepdims=True)
        ol = (o - oh.astype(_F32)).astype(_BF16)
        rh = rh_ref[cols, :]
        logits = (logits + jnp.dot(oh, rh, preferred_element_type=_F32)
                  + jnp.dot(oh, rl_ref[cols, :], preferred_element_type=_F32)
                  + jnp.dot(ol, rh, preferred_element_type=_F32))

    rs = lax.rsqrt(ssq * (1.0 / D) + NORM_EPS)
    rs_ref[...] = jnp.broadcast_to(rs, rs_ref.shape)
    logits = logits * rs
    lane = lax.broadcasted_iota(jnp.int32, (ts, LANES), 1).astype(_F32)
    neg = jnp.float32(-jnp.inf)
    logits = jnp.where(lane < n_experts, logits, neg)
    m1 = jnp.max(logits, axis=-1, keepdims=True)
    i1 = jnp.min(jnp.where(logits == m1, lane, float(LANES)), axis=-1, keepdims=True)
    rest = jnp.where(lane == i1, neg, logits)
    m2 = jnp.max(rest, axis=-1, keepdims=True)
    i2 = jnp.min(jnp.where(rest == m2, lane, float(LANES)), axis=-1, keepdims=True)
    e2 = jnp.exp(m2 - m1)
    g1 = 1.0 / (1.0 + e2)
    gt_ref[...] = jnp.where(lane == i1, g1, 0.0) + jnp.where(lane == i2, e2 * g1, 0.0)


def _pool_mix(x, pool_w, pool_scale, r_hi, r_lo, n_experts, *, ts):
    B, S, D = x.shape
    ts = min(ts, S)
    n = S // ts
    tpb = ts // HALO
    G, gs, _ = pool_w.shape
    kern = functools.partial(_pool_kernel, ts=ts, n_chunks=n, seq=S, n_experts=n_experts)
    return pl.pallas_call(
        kern,
        out_shape=(jax.ShapeDtypeStruct((B, S, D), _F32), jax.ShapeDtypeStruct((B, S, D), _BF16),
                   jax.ShapeDtypeStruct((B, S, LANES), _F32), jax.ShapeDtypeStruct((B, S, LANES), _F32)),
        grid=(B, n),
        in_specs=[
            pl.BlockSpec((None, ts, D), lambda b, t: (b, t, 0)),
            pl.BlockSpec((None, HALO, D), lambda b, t: (b, jnp.maximum(t * tpb - 1, 0), 0)),
            pl.BlockSpec((None, HALO, D), lambda b, t: (b, jnp.minimum((t + 1) * tpb, S // HALO - 1), 0)),
            pl.BlockSpec((G, gs, gs), lambda b, t: (0, 0, 0)),
            pl.BlockSpec((1, D), lambda b, t: (0, 0)),
            pl.BlockSpec((D, LANES), lambda b, t: (0, 0)),
            pl.BlockSpec((D, LANES), lambda b, t: (0, 0)),
        ],
        out_specs=(pl.BlockSpec((None, ts, D), lambda b, t: (b, t, 0)),
                   pl.BlockSpec((None, ts, D), lambda b, t: (b, t, 0)),
                   pl.BlockSpec((None, ts, LANES), lambda b, t: (b, t, 0)),
                   pl.BlockSpec((None, ts, LANES), lambda b, t: (b, t, 0))),
        scratch_shapes=[pltpu.VMEM((ts + 2 * HALO, D), _F32)],
        compiler_params=_params("parallel", "parallel"),
        name="pool_mix",
    )(x, x, x, pool_w, pool_scale, r_hi, r_lo)


def _res_norm_kernel(y_ref, w_ref, r_ref, g_ref, o_ref, *, tn, rb):
    j = pl.program_id(1)
    c0 = pl.multiple_of(j * tn, tn)
    o_ref[:, pl.ds(c0, tn)] = r_ref[...] + jnp.dot(y_ref[...], w_ref[...], preferred_element_type=_F32)

    @pl.when(j == pl.num_programs(1) - 1)
    def _():
        def body(s, _):
            r0 = pl.multiple_of(s * rb, rb)
            x = o_ref[pl.ds(r0, rb), :]
            o_ref[pl.ds(r0, rb), :] = (x * lax.rsqrt(jnp.mean(x * x, axis=-1, keepdims=True) + NORM_EPS)
                                       * g_ref[...])
            return 0

        lax.fori_loop(0, o_ref.shape[0] // rb, body, 0)


def _res_norm_mm(y, w, res, gain, *, tm, tn):
    T, K = y.shape
    N = w.shape[1]
    tm, tn = min(tm, T), min(tn, N)
    rb = min(64, tm)
    return pl.pallas_call(
        functools.partial(_res_norm_kernel, tn=tn, rb=rb),
        out_shape=jax.ShapeDtypeStruct((T, N), _F32),
        grid=(T // tm, N // tn),
        in_specs=[pl.BlockSpec((tm, K), lambda i, j: (i, 0)),
                  pl.BlockSpec((K, tn), lambda i, j: (0, j)),
                  pl.BlockSpec((tm, tn), lambda i, j: (i, j)),
                  pl.BlockSpec((1, N), lambda i, j: (0, 0))],
        out_specs=pl.BlockSpec((tm, N), lambda i, j: (i, 0)),
        compiler_params=_params("parallel", "arbitrary"),
        name="res_norm_mm",
    )(y, w, res, gain)


def _fold(gain, w):
    return (gain[:, None] * w).astype(_BF16)


def _prep_weights(mix_norm, ffn_norm, final_norm, rg_w_in, rg_conv_w, rg_conv_b, rg_w_a, rg_b_a, rg_w_i,
                  rg_b_i, rg_lambda, rg_w_out, pool_w, pool_scale, ffn_w_gate, ffn_w_up, ffn_w_down,
                  moe_router, moe_w_gate, moe_w_up, moe_w_down):
    depth, D = mix_norm.shape
    layers = []
    for i in range(depth):
        j = i // 2
        lw = {}
        if i % 2 == 0:
            lw["w_in"] = _fold(mix_norm[i], rg_w_in[j])
            lw["conv_w"] = 0.5 * rg_conv_w[j]
            lw["conv_b"] = 0.5 * rg_conv_b[j][None, :]
            lw["w_a"] = rg_w_a[j].astype(_BF16)
            lw["b_a"] = 0.5 * rg_b_a[j][:, None, :]
            lw["w_i"] = rg_w_i[j].astype(_BF16)
            lw["b_i"] = 0.5 * rg_b_i[j][:, None, :]
            lw["cneg"] = (-RG_C * jax.nn.softplus(-rg_lambda[j]))[:, None, :]
            lw["w_out"] = rg_w_out[j].astype(_BF16)
            lw["w_gate"] = _fold(ffn_norm[i], ffn_w_gate[j])
            lw["w_up"] = _fold(ffn_norm[i], ffn_w_up[j])
            lw["w_down"] = ffn_w_down[j].astype(_BF16)
        else:
            E = moe_router.shape[-1]
            G, gs = pool_w.shape[1:3]
            lw["pool_w"] = (mix_norm[i].reshape(G, gs, 1) * pool_w[j]).astype(_BF16)
            lw["pool_scale"] = pool_scale[j][None, :]
            router = jnp.pad(ffn_norm[i][:, None] * moe_router[j], ((0, 0), (0, LANES - E)))
            lw["r_hi"] = router.astype(_BF16)
            lw["r_lo"] = (router - lw["r_hi"].astype(_F32)).astype(_BF16)
            lw["n_experts"] = E
            lw["w_gate"] = _fold(ffn_norm[i], moe_w_gate[j])
            lw["w_up"] = _fold(ffn_norm[i], moe_w_up[j])
            lw["w_down"] = moe_w_down[j].reshape(-1, D).astype(_BF16)
        layers.append(lw)
    return layers, final_norm[None, :]


def _trunk(x, layers, final_gain, *, tm, tn, rg_ts, pool_ts):
    B, S, D = x.shape
    T = B * S
    xf = x.reshape(T, D)
    xb, rs = _cast_rstd(xf, tm=tm // 4)
    for i, lw in enumerate(layers):
        if i % 2 == 0:
            proj = _proj_mm(xb, rs, lw["w_in"], tm=tm, tn=2 * tn)
            y = _rg_core(proj.reshape(B, S, 2 * D), lw["conv_w"], lw["conv_b"], lw["w_a"], lw["b_a"],
                         lw["w_i"], lw["b_i"], lw["cneg"], ts=rg_ts)
            xf, xb, rs = _res_mm(y.reshape(T, D), lw["w_out"], xf, tm=tm, tn=tn)
            h = _glu_mm(xb, rs, lw["w_gate"], lw["w_up"], tm=tm, tn=tn)
        else:
            o, ob, gates, rs = _pool_mix(xf.reshape(B, S, D), lw["pool_w"], lw["pool_scale"],
                                         lw["r_hi"], lw["r_lo"], lw["n_experts"], ts=pool_ts)
            xf, xb, rs = o.reshape(T, D), ob.reshape(T, D), rs.reshape(T, LANES)
            h = _moe_glu_mm(xb, rs, lw["w_gate"], lw["w_up"], gates.reshape(T, LANES), tm=tm)
        if i + 1 < len(layers):
            xf, xb, rs = _res_mm(h, lw["w_down"], xf, tm=tm, tn=tn)
    return _res_norm_mm(h, lw["w_down"], xf, final_gain, tm=tm // 2, tn=tn).reshape(B, S, D)


def kernel(x_prompt, x_sample, mix_norm, ffn_norm, final_norm, rg_w_in, rg_conv_w, rg_conv_b, rg_w_a, rg_b_a, rg_w_i, rg_b_i, rg_lambda, rg_w_out, pool_w, pool_scale, ffn_w_gate, ffn_w_up, ffn_w_down, moe_router, moe_w_gate, moe_w_up, moe_w_down):
    layers, final_gain = _prep_weights(
        mix_norm, ffn_norm, final_norm, rg_w_in, rg_conv_w, rg_conv_b, rg_w_a, rg_b_a, rg_w_i, rg_b_i,
        rg_lambda, rg_w_out, pool_w, pool_scale, ffn_w_gate, ffn_w_up, ffn_w_down, moe_router,
        moe_w_gate, moe_w_up, moe_w_down)
    y_prompt = _trunk(x_prompt, layers, final_gain, **_tiles(x_prompt.shape[1]))
    y_sample = _trunk(x_sample, layers, final_gain, **_tiles(x_sample.shape[1]))
    return (y_prompt, y_sample)
```

```python
import functools
import math

import jax
import jax.numpy as jnp
from jax import lax
from jax.experimental import pallas as pl
from jax.experimental.pallas import tpu as pltpu

NORM_EPS = 1e-6
RG_C = 8.0
RG_BLOCK = 256
CONV_WIDTH = 4
CONV_PAD_LEFT = (CONV_WIDTH - 1) // 2
POOL_WINDOWS = (2, 4, 8, 16)
TOP_K = 2
SUBLANES = 8
LANES = 128
HALO = 8
VMEM_LIMIT = 56 * 1024 * 1024
LOG2E = 1.4426950408889634
GELU_C1 = math.sqrt(2.0 / math.pi)
GELU_C3 = 0.044715 * GELU_C1
TINY = 1e-30
SCAN_LC = 64

_BF16 = jnp.bfloat16
_F32 = jnp.float32


def _tiles(seq):
    rg_ts = min(seq, 2048)
    while (2 * seq + 10 * rg_ts) * RG_BLOCK * 4 > VMEM_LIMIT - (3 << 20) and rg_ts > 8 * SCAN_LC:
        rg_ts //= 2
    return dict(tm=1024, tn=512, rg_ts=rg_ts, pool_ts=256)


def _params(*sem):
    return pltpu.CompilerParams(dimension_semantics=sem, vmem_limit_bytes=VMEM_LIMIT)


def _cast_rstd_kernel(x_ref, xb_ref, rs_ref):
    x = x_ref[...]
    xb_ref[...] = x.astype(_BF16)
    rs = lax.rsqrt(jnp.mean(x * x, axis=-1, keepdims=True) + NORM_EPS)
    rs_ref[...] = jnp.broadcast_to(rs, rs_ref.shape)


def _scale_rows(v, rs):
    return v * jnp.concatenate([rs] * (v.shape[1] // LANES), axis=1)


def _cast_rstd(x, *, tm):
    T, K = x.shape
    tm = min(tm, T)
    return pl.pallas_call(
        _cast_rstd_kernel,
        out_shape=(jax.ShapeDtypeStruct((T, K), _BF16), jax.ShapeDtypeStruct((T, LANES), _F32)),
        grid=(T // tm,),
        in_specs=[pl.BlockSpec((tm, K), lambda i: (i, 0))],
        out_specs=(pl.BlockSpec((tm, K), lambda i: (i, 0)), pl.BlockSpec((tm, LANES), lambda i: (i, 0))),
        compiler_params=_params("parallel"),
        name="cast_rstd",
    )(x)


def _sigmoid(x):
    return 1.0 / (1.0 + jnp.exp2(x * (-LOG2E)))


def _gelu_tanh(x):
    inner = x * (GELU_C1 + GELU_C3 * (x * x))
    return (0.5 * x) * (1.0 + jnp.tanh(inner))


def _proj_kernel(x_ref, rs_ref, w_ref, o_ref):
    o_ref[...] = _scale_rows(jnp.dot(x_ref[...], w_ref[...], preferred_element_type=_F32), rs_ref[...])


def _proj_mm(xb, rs, w, *, tm, tn):
    T, K = xb.shape
    N = w.shape[1]
    tm, tn = min(tm, T), min(tn, N)
    return pl.pallas_call(
        _proj_kernel,
        out_shape=jax.ShapeDtypeStruct((T, N), _F32),
        grid=(T // tm, N // tn),
        in_specs=[pl.BlockSpec((tm, K), lambda i, j: (i, 0)),
                  pl.BlockSpec((tm, LANES), lambda i, j: (i, 0)),
                  pl.BlockSpec((K, tn), lambda i, j: (0, j))],
        out_specs=pl.BlockSpec((tm, tn), lambda i, j: (i, j)),
        compiler_params=_params("parallel", "parallel"),
        name="proj_mm",
    )(xb, rs, w)


def _res_kernel(y_ref, w_ref, r_ref, o_ref, ob_ref, rs_ref, *, n_cols):
    j = pl.program_id(1)

    @pl.when(j == 0)
    def _():
        rs_ref[...] = jnp.zeros_like(rs_ref)

    o = r_ref[...] + jnp.dot(y_ref[...], w_ref[...], preferred_element_type=_F32)
    o_ref[...] = o
    ob_ref[...] = o.astype(_BF16)
    o2 = o * o
    part = o2[:, 0:LANES]
    for c in range(1, o.shape[1] // LANES):
        part = part + o2[:, c * LANES:(c + 1) * LANES]
    rs_ref[...] += part

    @pl.when(j == pl.num_programs(1) - 1)
    def _():
        ssq = jnp.sum(rs_ref[...], axis=-1, keepdims=True)
        rs_ref[...] = jnp.broadcast_to(lax.rsqrt(ssq * (1.0 / n_cols) + NORM_EPS), rs_ref.shape)


def _res_mm(y, w, res, *, tm, tn):
    T, K = y.shape
    N = w.shape[1]
    tm, tn = min(tm, T), min(tn, N)
    return pl.pallas_call(
        functools.partial(_res_kernel, n_cols=N),
        out_shape=(jax.ShapeDtypeStruct((T, N), _F32), jax.ShapeDtypeStruct((T, N), _BF16),
                   jax.ShapeDtypeStruct((T, LANES), _F32)),
        grid=(T // tm, N // tn),
        in_specs=[pl.BlockSpec((tm, K), lambda i, j: (i, 0)),
                  pl.BlockSpec((K, tn), lambda i, j: (0, j)),
                  pl.BlockSpec((tm, tn), lambda i, j: (i, j))],
        out_specs=(pl.BlockSpec((tm, tn), lambda i, j: (i, j)),
                   pl.BlockSpec((tm, tn), lambda i, j: (i, j)),
                   pl.BlockSpec((tm, LANES), lambda i, j: (i, 0))),
        compiler_params=_params("parallel", "arbitrary"),
        name="res_mm",
    )(y, w, res)


def _glu_kernel(x_ref, rs_ref, wg_ref, wu_ref, o_ref):
    rs = rs_ref[...]
    g = _scale_rows(jnp.dot(x_ref[...], wg_ref[...], preferred_element_type=_F32), rs)
    u = _scale_rows(jnp.dot(x_ref[...], wu_ref[...], preferred_element_type=_F32), rs)
    o_ref[...] = (g * _sigmoid(g) * u).astype(o_ref.dtype)


def _glu_mm(xb, rs, wg, wu, *, tm, tn):
    T, K = xb.shape
    N = wg.shape[1]
    tm, tn = min(tm, T), min(tn, N)
    return pl.pallas_call(
        _glu_kernel,
        out_shape=jax.ShapeDtypeStruct((T, N), _BF16),
        grid=(T // tm, N // tn),
        in_specs=[pl.BlockSpec((tm, K), lambda i, j: (i, 0)),
                  pl.BlockSpec((tm, LANES), lambda i, j: (i, 0)),
                  pl.BlockSpec((K, tn), lambda i, j: (0, j)),
                  pl.BlockSpec((K, tn), lambda i, j: (0, j))],
        out_specs=pl.BlockSpec((tm, tn), lambda i, j: (i, j)),
        compiler_params=_params("parallel", "parallel"),
        name="glu_mm",
    )(xb, rs, wg, wu)


def _moe_glu_kernel(x_ref, rs_ref, wg_ref, wu_ref, gt_ref, o_ref):
    e = pl.program_id(1)
    rs = rs_ref[...]
    g = _scale_rows(jnp.dot(x_ref[...], wg_ref[...], preferred_element_type=_F32), rs)
    u = _scale_rows(jnp.dot(x_ref[...], wu_ref[...], preferred_element_type=_F32), rs)
    gates = gt_ref[...]
    lane = lax.broadcasted_iota(jnp.int32, gates.shape, 1)
    ge = jnp.sum(jnp.where(lane == e, gates, 0.0), axis=-1, keepdims=True)
    o_ref[...] = (g * _sigmoid(g) * u * ge).astype(o_ref.dtype)


def _moe_glu_mm(xb, rs, wg, wu, gates, *, tm):
    T, K = xb.shape
    E, _, F = wg.shape
    tm = min(tm, T)
    return pl.pallas_call(
        _moe_glu_kernel,
        out_shape=jax.ShapeDtypeStruct((T, E * F), _BF16),
        grid=(T // tm, E),
        in_specs=[pl.BlockSpec((tm, K), lambda i, e: (i, 0)),
                  pl.BlockSpec((tm, LANES), lambda i, e: (i, 0)),
                  pl.BlockSpec((None, K, F), lambda i, e: (e, 0, 0)),
                  pl.BlockSpec((None, K, F), lambda i, e: (e, 0, 0)),
                  pl.BlockSpec((tm, LANES), lambda i, e: (i, 0))],
        out_specs=pl.BlockSpec((tm, F), lambda i, e: (i, e)),
        compiler_params=_params("parallel", "parallel"),
        name="moe_glu_mm",
    )(xb, rs, wg, wu, gates)


def _rg_kernel(rec_ref, prev_ref, next_ref, gate_ref, cw_ref, cb_ref, wa_ref, ba_ref, wi_ref, bi_ref,
               cn_ref, y_ref, hs_ref, us_ref, xi_ref, a_ref, b_ref, hl_ref, pl_ref, carry_ref,
               *, ts, n_chunks, lc, sb):
    p = pl.program_id(2)
    k = pl.program_id(3)
    t = jnp.where(p == 0, k, n_chunks - 1 - k)
    base = pl.multiple_of(t * ts, ts)
    C = rec_ref.shape[-1]
    G = C // LANES
    R = ts // (SUBLANES * lc)
    me = lc + CONV_WIDTH - 1
    nk = sb // SUBLANES
    per_stream = lc // nk
    row = lax.broadcasted_iota(jnp.int32, (SUBLANES, LANES), 0)
    streams = [(r, g) for r in range(R) for g in range(G)]

    @pl.when(k == 0)
    def _():
        carry_ref[...] = jnp.zeros_like(carry_ref)

    ba = ba_ref[...]
    bi = bi_ref[...]
    ch = cn_ref[...] * (0.5 * LOG2E)

    def lanes(g):
        return slice(g * LANES, (g + 1) * LANES)

    def gates(uh, r0):
        ub = uh.astype(_BF16)
        ta = jnp.tanh(jnp.dot(ub, wa_ref[...], preferred_element_type=_F32) + ba)
        ti = jnp.tanh(jnp.dot(ub, wi_ref[...], preferred_element_type=_F32) + bi)
        a = jnp.exp2(ch + ch * ta)
        x = 1.0 - a * a
        b = x * lax.rsqrt(jnp.maximum(x, TINY)) * (1.0 + ti) * uh
        for g in range(G):
            a_ref[g, pl.ds(r0, sb), :] = a[:, lanes(g)]
            b_ref[g, pl.ds(r0, sb), :] = b[:, lanes(g)]

    def piece(g, q):
        r, j = q // SUBLANES, q % SUBLANES
        return (g, pl.ds(r * (lc * SUBLANES) + j, lc, stride=SUBLANES), slice(None))

    def tile(g, r, k8):
        return (g, pl.ds(k8 + r * (lc * SUBLANES), SUBLANES), slice(None))

    def local_scan(reverse):
        def body(i, hp):
            k8 = pl.multiple_of(((lc - 1 - i) if reverse else i) * SUBLANES, SUBLANES)
            out = []
            for n, (r, g) in enumerate(streams):
                at = a_ref[tile(g, r, k8)]
                h = at * hp[2 * n] + b_ref[tile(g, r, k8)]
                pr = at * hp[2 * n + 1]
                hl_ref[tile(g, r, k8)] = h
                pl_ref[tile(g, r, k8)] = pr
                out += [h, pr]
            return tuple(out)

        init = (jnp.zeros((SUBLANES, LANES), _F32), jnp.ones((SUBLANES, LANES), _F32)) * len(streams)
        return lax.fori_loop(0, lc, body, init, unroll=2)

    def sub_chunk_states(hp, reverse):
        cvecs = {}
        edge = 0 if reverse else SUBLANES - 1
        for g in range(G):
            cur = carry_ref[g]
            for r in (range(R - 1, -1, -1) if reverse else range(R)):
                n = streams.index((r, g))
                b, a = hp[2 * n], hp[2 * n + 1]
                for d in (1, 2, 4):
                    shift, valid = (SUBLANES - d, row < SUBLANES - d) if reverse else (d, row >= d)
                    b = b + a * jnp.where(valid, pltpu.roll(b, shift, 0), 0.0)
                    a = a * jnp.where(valid, pltpu.roll(a, shift, 0), 1.0)
                after = b + a * cur
                first = row == (SUBLANES - 1 if reverse else 0)
                cvecs[(r, g)] = jnp.where(first, cur, pltpu.roll(after, SUBLANES - 1 if reverse else 1, 0))
                cur = jnp.broadcast_to(after[edge:edge + 1, :], (SUBLANES, LANES))
            carry_ref[g] = cur
        return cvecs

    def sub_block(s):
        r, h = s // per_stream, s % per_stream
        return (pl.multiple_of((r * lc + h * nk) * SUBLANES, SUBLANES),
                pl.multiple_of((r * me + h * nk) * SUBLANES, SUBLANES))

    @pl.when(p == 0)
    def _():
        for q in range(SUBLANES * R):
            r, j = divmod(q, SUBLANES)
            for g in range(G):
                xi_ref[g, pl.ds((r * me + 1) * SUBLANES + j, lc, stride=SUBLANES), :] = (
                    rec_ref[pl.ds(q * lc, lc), lanes(g)])

        def step_tile(g, r, kk):
            return xi_ref[g, pl.ds((r * me + 1 + kk) * SUBLANES, SUBLANES), :]

        def bcast(v):
            return jnp.broadcast_to(v, (SUBLANES, LANES))

        for g in range(G):
            before = jnp.where(t > 0, prev_ref[HALO - 1:HALO, lanes(g)], 0.0)
            after = jnp.where(t < n_chunks - 1, next_ref[0:2, lanes(g)], 0.0)
            for r in range(R):
                fill = step_tile(g, r - 1, lc - 1)[SUBLANES - 1:SUBLANES, :] if r > 0 else before
                xi_ref[g, pl.ds(r * me * SUBLANES, SUBLANES), :] = jnp.where(
                    row == 0, bcast(fill), pltpu.roll(step_tile(g, r, lc - 1), 1, 0))
                for d in range(2):
                    fill = step_tile(g, r + 1, d)[0:1, :] if r + 1 < R else after[d:d + 1, :]
                    xi_ref[g, pl.ds((r * me + 1 + lc + d) * SUBLANES, SUBLANES), :] = jnp.where(
                        row == SUBLANES - 1, bcast(fill), pltpu.roll(step_tile(g, r, d), SUBLANES - 1, 0))

        cw = cw_ref[...]
        cb = cb_ref[...]

        def gates_body(s, _):
            r0, x0 = sub_block(s)
            parts = []
            for g in range(G):
                win = xi_ref[g, pl.ds(x0, sb + (CONV_WIDTH - 1) * SUBLANES), :]
                uh = cb[:, lanes(g)]
                for tap in range(CONV_WIDTH):
                    uh = uh + cw[tap:tap + 1, lanes(g)] * win[tap * SUBLANES:tap * SUBLANES + sb, :]
                us_ref[g, pl.ds(base + r0, sb), :] = uh
                parts.append(uh)
            gates(jnp.concatenate(parts, axis=1), r0)
            return 0

        lax.fori_loop(0, ts // sb, gates_body, 0, unroll=2)
        cvecs = sub_chunk_states(local_scan(False), False)

        def fix_body(kk, _):
            k8 = pl.multiple_of(kk * SUBLANES, SUBLANES)
            for r, g in streams:
                hs_ref[tile(g, r, base + k8)] = hl_ref[tile(g, r, k8)] + pl_ref[tile(g, r, k8)] * cvecs[(r, g)]
            return 0

        lax.fori_loop(0, lc, fix_body, 0, unroll=2)

    @pl.when(p == 1)
    def _():
        def gates_body(s, _):
            r0, _ = sub_block(s)
            gates(jnp.concatenate([us_ref[g, pl.ds(base + r0, sb), :] for g in range(G)], axis=1), r0)
            return 0

        lax.fori_loop(0, ts // sb, gates_body, 0, unroll=4)
        cvecs = sub_chunk_states(local_scan(True), True)

        def fix_body(kk, _):
            k8 = pl.multiple_of(kk * SUBLANES, SUBLANES)
            for r, g in streams:
                h = hl_ref[tile(g, r, k8)] + pl_ref[tile(g, r, k8)] * cvecs[(r, g)]
                a_ref[tile(g, r, k8)] = h + hs_ref[tile(g, r, base + k8)]
            return 0

        lax.fori_loop(0, lc, fix_body, 0, unroll=2)

        def out_body(s, _):
            r0 = pl.multiple_of(s * sb, sb)
            gate = _gelu_tanh(gate_ref[pl.ds(r0, sb), :])
            for i in range(sb // lc):
                for g in range(G):
                    y_ref[pl.ds(r0 + i * lc, lc), lanes(g)] = (
                        a_ref[piece(g, s * (sb // lc) + i)] * gate[i * lc:(i + 1) * lc, lanes(g)]
                    ).astype(y_ref.dtype)
            return 0

        lax.fori_loop(0, ts // sb, out_body, 0)


def _rg_core(proj, conv_w, conv_b, w_a, b_a, w_i, b_i, cneg, *, ts, lc=SCAN_LC):
    B, S, D2 = proj.shape
    D = D2 // 2
    C = RG_BLOCK
    G = C // LANES
    nc = D // C
    ts = min(ts, S)
    n = S // ts
    lc = min(lc, ts // SUBLANES)
    sb = min(256, lc * SUBLANES)
    assert lc % SUBLANES == 0 and ts % (SUBLANES * lc) == 0 and (lc * SUBLANES) % sb == 0 and sb % lc == 0
    assert (CONV_WIDTH, CONV_PAD_LEFT) == (4, 1)
    tpb = ts // HALO

    def t_of(p, k):
        return jnp.where(p == 0, k, n - 1)

    def t_out(p, k):
        return jnp.where(p == 0, n - 1, n - 1 - k)

    kern = functools.partial(_rg_kernel, ts=ts, n_chunks=n, lc=lc, sb=sb)
    return pl.pallas_call(
        kern,
        out_shape=jax.ShapeDtypeStruct((B, S, D), _BF16),
        grid=(B, nc, 2, n),
        in_specs=[
            pl.BlockSpec((None, ts, C), lambda b, c, p, k: (b, t_of(p, k), nc + c)),
            pl.BlockSpec((None, HALO, C), lambda b, c, p, k: (b, jnp.maximum(t_of(p, k) * tpb - 1, 0), nc + c)),
            pl.BlockSpec((None, HALO, C),
                         lambda b, c, p, k: (b, jnp.minimum((t_of(p, k) + 1) * tpb, S // HALO - 1), nc + c)),
            pl.BlockSpec((None, ts, C), lambda b, c, p, k: (b, t_out(p, k), c)),
            pl.BlockSpec((CONV_WIDTH, C), lambda b, c, p, k: (0, c)),
            pl.BlockSpec((1, C), lambda b, c, p, k: (0, c)),
            pl.BlockSpec((None, None, C, C), lambda b, c, p, k: (p, c, 0, 0)),
            pl.BlockSpec((None, 1, C), lambda b, c, p, k: (p, 0, c)),
            pl.BlockSpec((None, None, C, C), lambda b, c, p, k: (p, c, 0, 0)),
            pl.BlockSpec((None, 1, C), lambda b, c, p, k: (p, 0, c)),
            pl.BlockSpec((None, 1, C), lambda b, c, p, k: (p, 0, c)),
        ],
        out_specs=pl.BlockSpec((None, ts, C), lambda b, c, p, k: (b, t_out(p, k), c)),
        scratch_shapes=[
            pltpu.VMEM((G, S, LANES), _F32),
            pltpu.VMEM((G, S, LANES), _F32),
            pltpu.VMEM((G, (ts // lc // SUBLANES) * (lc + CONV_WIDTH - 1) * SUBLANES, LANES), _F32),
            pltpu.VMEM((G, ts, LANES), _F32),
            pltpu.VMEM((G, ts, LANES), _F32),
            pltpu.VMEM((G, ts, LANES), _F32),
            pltpu.VMEM((G, ts, LANES), _F32),
            pltpu.VMEM((G, SUBLANES, LANES), _F32),
        ],
        compiler_params=_params("parallel", "parallel", "arbitrary", "arbitrary"),
        name="rg_core",
    )(proj, proj, proj, proj, conv_w, conv_b, w_a, b_a, w_i, b_i, cneg)


def _pool_kernel(x_ref, prev_ref, next_ref, pw_ref, ps_ref, rh_ref, rl_ref,
                 o_ref, ob_ref, gt_ref, rs_ref, ext_ref, *, ts, n_chunks, seq, n_experts):
    t = pl.program_id(1)
    D = x_ref.shape[-1]
    G = len(POOL_WINDOWS)
    gs = D // G
    L = ts + 2 * HALO
    assert POOL_WINDOWS[-1] == 2 * HALO and all(w <= HALO for w in POOL_WINDOWS[:-1])

    def scaled(v):
        return v * lax.rsqrt(jnp.mean(v * v, axis=-1, keepdims=True) + NORM_EPS)

    ext_ref[pl.ds(HALO, ts), :] = scaled(x_ref[...])
    ext_ref[pl.ds(0, HALO), :] = jnp.where(t > 0, scaled(prev_ref[...]), 0.0)
    ext_ref[pl.ds(HALO + ts, HALO), :] = jnp.where(t < n_chunks - 1, scaled(next_ref[...]), 0.0)

    def ahead(v, n):
        return pltpu.roll(v, L - n, 0)

    tpos = t * ts + lax.broadcasted_iota(jnp.int32, (ts, 1), 0)
    ssq = jnp.zeros((ts, 1), _F32)
    logits = jnp.zeros((ts, LANES), _F32)
    for gi, win in enumerate(POOL_WINDOWS):
        cols = pl.ds(gi * gs, gs)
        e = ext_ref[:, cols]
        if win == 2:
            s = (pltpu.roll(e, 1, 0) + e)[HALO:HALO + ts, :]
        else:
            f, w = e, 1
            while w < min(win, HALO):
                f = f + ahead(f, w)
                w *= 2
            if win < 2 * HALO:
                s = pltpu.roll(f, win // 2, 0)[HALO:HALO + ts, :]
            else:
                s = f[0:ts, :] + f[HALO:HALO + ts, :]
        half = win // 2
        cnt = jnp.minimum(tpos + (win - half), seq) - jnp.maximum(tpos - half, 0)
        d = s * (1.0 / cnt.astype(_F32)) - e[HALO:HALO + ts, :]
        y = jnp.dot(d.astype(_BF16), pw_ref[gi], preferred_element_type=_F32)
        o = x_ref[:, cols] + y * ps_ref[:, cols]
        o_ref[:, cols] = o
        oh = o.astype(_BF16)
        ob_ref[:, cols] = oh
        ssq = ssq + jnp.sum(o * o, axis=-1, keepdims=True)
        ol = (o - oh.astype(_F32)).astype(_BF16)
        rh = rh_ref[cols, :]
        logits = (logits + jnp.dot(oh, rh, preferred_element_type=_F32)
                  + jnp.dot(oh, rl_ref[cols, :], preferred_element_type=_F32)
                  + jnp.dot(ol, rh, preferred_element_type=_F32))

    rs = lax.rsqrt(ssq * (1.0 / D) + NORM_EPS)
    rs_ref[...] = jnp.broadcast_to(rs, rs_ref.shape)
    logits = logits * rs
    lane = lax.broadcasted_iota(jnp.int32, (ts, LANES), 1).astype(_F32)
    neg = jnp.float32(-jnp.inf)
    logits = jnp.where(lane < n_experts, logits, neg)
    m1 = jnp.max(logits, axis=-1, keepdims=True)
    i1 = jnp.min(jnp.where(logits == m1, lane, float(LANES)), axis=-1, keepdims=True)
    rest = jnp.where(lane == i1, neg, logits)
    m2 = jnp.max(rest, axis=-1, keepdims=True)
    i2 = jnp.min(jnp.where(rest == m2, lane, float(LANES)), axis=-1, keepdims=True)
    e2 = jnp.exp(m2 - m1)
    g1 = 1.0 / (1.0 + e2)
    gt_ref[...] = jnp.where(lane == i1, g1, 0.0) + jnp.where(lane == i2, e2 * g1, 0.0)


def _pool_mix(x, pool_w, pool_scale, r_hi, r_lo, n_experts, *, ts):
    B, S, D = x.shape
    ts = min(ts, S)
    n = S // ts
    tpb = ts // HALO
    G, gs, _ = pool_w.shape
    kern = functools.partial(_pool_kernel, ts=ts, n_chunks=n, seq=S, n_experts=n_experts)
    return pl.pallas_call(
        kern,
        out_shape=(jax.ShapeDtypeStruct((B, S, D), _F32), jax.ShapeDtypeStruct((B, S, D), _BF16),
                   jax.ShapeDtypeStruct((B, S, LANES), _F32), jax.ShapeDtypeStruct((B, S, LANES), _F32)),
        grid=(B, n),
        in_specs=[
            pl.BlockSpec((None, ts, D), lambda b, t: (b, t, 0)),
            pl.BlockSpec((None, HALO, D), lambda b, t: (b, jnp.maximum(t * tpb - 1, 0), 0)),
            pl.BlockSpec((None, HALO, D), lambda b, t: (b, jnp.minimum((t + 1) * tpb, S // HALO - 1), 0)),
            pl.BlockSpec((G, gs, gs), lambda b, t: (0, 0, 0)),
            pl.BlockSpec((1, D), lambda b, t: (0, 0)),
            pl.BlockSpec((D, LANES), lambda b, t: (0, 0)),
            pl.BlockSpec((D, LANES), lambda b, t: (0, 0)),
        ],
        out_specs=(pl.BlockSpec((None, ts, D), lambda b, t: (b, t, 0)),
                   pl.BlockSpec((None, ts, D), lambda b, t: (b, t, 0)),
                   pl.BlockSpec((None, ts, LANES), lambda b, t: (b, t, 0)),
                   pl.BlockSpec((None, ts, LANES), lambda b, t: (b, t, 0))),
        scratch_shapes=[pltpu.VMEM((ts + 2 * HALO, D), _F32)],
        compiler_params=_params("parallel", "parallel"),
        name="pool_mix",
    )(x, x, x, pool_w, pool_scale, r_hi, r_lo)


def _res_norm_kernel(y_ref, w_ref, r_ref, g_ref, o_ref, *, tn, rb):
    j = pl.program_id(1)
    c0 = pl.multiple_of(j * tn, tn)
    o_ref[:, pl.ds(c0, tn)] = r_ref[...] + jnp.dot(y_ref[...], w_ref[...], preferred_element_type=_F32)

    @pl.when(j == pl.num_programs(1) - 1)
    def _():
        def body(s, _):
            r0 = pl.multiple_of(s * rb, rb)
            x = o_ref[pl.ds(r0, rb), :]
            o_ref[pl.ds(r0, rb), :] = (x * lax.rsqrt(jnp.mean(x * x, axis=-1, keepdims=True) + NORM_EPS)
                                       * g_ref[...])
            return 0

        lax.fori_loop(0, o_ref.shape[0] // rb, body, 0)


def _res_norm_mm(y, w, res, gain, *, tm, tn):
    T, K = y.shape
    N = w.shape[1]
    tm, tn = min(tm, T), min(tn, N)
    rb = min(64, tm)
    return pl.pallas_call(
        functools.partial(_res_norm_kernel, tn=tn, rb=rb),
        out_shape=jax.ShapeDtypeStruct((T, N), _F32),
        grid=(T // tm, N // tn),
        in_specs=[pl.BlockSpec((tm, K), lambda i, j: (i, 0)),
                  pl.BlockSpec((K, tn), lambda i, j: (0, j)),
                  pl.BlockSpec((tm, tn), lambda i, j: (i, j)),
                  pl.BlockSpec((1, N), lambda i, j: (0, 0))],
        out_specs=pl.BlockSpec((tm, N), lambda i, j: (i, 0)),
        compiler_params=_params("parallel", "arbitrary"),
        name="res_norm_mm",
    )(y, w, res, gain)


def _fold(gain, w):
    return (gain[:, None] * w).astype(_BF16)


def _prep_weights(mix_norm, ffn_norm, final_norm, rg_w_in, rg_conv_w, rg_conv_b, rg_w_a, rg_b_a, rg_w_i,
                  rg_b_i, rg_lambda, rg_w_out, pool_w, pool_scale, ffn_w_gate, ffn_w_up, ffn_w_down,
                  moe_router, moe_w_gate, moe_w_up, moe_w_down):
    depth, D = mix_norm.shape
    layers = []
    for i in range(depth):
        j = i // 2
        lw = {}
        if i % 2 == 0:
            lw["w_in"] = _fold(mix_norm[i], rg_w_in[j])
            lw["conv_w"] = 0.5 * rg_conv_w[j]
            lw["conv_b"] = 0.5 * rg_conv_b[j][None, :]
            lw["w_a"] = rg_w_a[j].astype(_BF16)
            lw["b_a"] = 0.5 * rg_b_a[j][:, None, :]
            lw["w_i"] = rg_w_i[j].astype(_BF16)
            lw["b_i"] = 0.5 * rg_b_i[j][:, None, :]
            lw["cneg"] = (-RG_C * jax.nn.softplus(-rg_lambda[j]))[:, None, :]
            lw["w_out"] = rg_w_out[j].astype(_BF16)
            lw["w_gate"] = _fold(ffn_norm[i], ffn_w_gate[j])
            lw["w_up"] = _fold(ffn_norm[i], ffn_w_up[j])
            lw["w_down"] = ffn_w_down[j].astype(_BF16)
        else:
            E = moe_router.shape[-1]
            G, gs = pool_w.shape[1:3]
            lw["pool_w"] = (mix_norm[i].reshape(G, gs, 1) * pool_w[j]).astype(_BF16)
            lw["pool_scale"] = pool_scale[j][None, :]
            router = jnp.pad(ffn_norm[i][:, None] * moe_router[j], ((0, 0), (0, LANES - E)))
            lw["r_hi"] = router.astype(_BF16)
            lw["r_lo"] = (router - lw["r_hi"].astype(_F32)).astype(_BF16)
            lw["n_experts"] = E
            lw["w_gate"] = _fold(ffn_norm[i], moe_w_gate[j])
            lw["w_up"] = _fold(ffn_norm[i], moe_w_up[j])
            lw["w_down"] = moe_w_down[j].reshape(-1, D).astype(_BF16)
        layers.append(lw)
    return layers, final_norm[None, :]


def _trunk(x, layers, final_gain, *, tm, tn, rg_ts, pool_ts):
    B, S, D = x.shape
    T = B * S
    xf = x.reshape(T, D)
    xb, rs = _cast_rstd(xf, tm=tm // 4)
    for i, lw in enumerate(layers):
        if i % 2 == 0:
            proj = _proj_mm(xb, rs, lw["w_in"], tm=tm, tn=2 * tn)
            y = _rg_core(proj.reshape(B, S, 2 * D), lw["conv_w"], lw["conv_b"], lw["w_a"], lw["b_a"],
                         lw["w_i"], lw["b_i"], lw["cneg"], ts=rg_ts)
            xf, xb, rs = _res_mm(y.reshape(T, D), lw["w_out"], xf, tm=tm, tn=tn)
            h = _glu_mm(xb, rs, lw["w_gate"], lw["w_up"], tm=tm, tn=tn)
        else:
            o, ob, gates, rs = _pool_mix(xf.reshape(B, S, D), lw["pool_w"], lw["pool_scale"],
                                         lw["r_hi"], lw["r_lo"], lw["n_experts"], ts=pool_ts)
            xf, xb, rs = o.reshape(T, D), ob.reshape(T, D), rs.reshape(T, LANES)
            h = _moe_glu_mm(xb, rs, lw["w_gate"], lw["w_up"], gates.reshape(T, LANES), tm=tm)
        if i + 1 < len(layers):
            xf, xb, rs = _res_mm(h, lw["w_down"], xf, tm=tm, tn=tn)
    return _res_norm_mm(h, lw["w_down"], xf, final_gain, tm=tm // 2, tn=tn).reshape(B, S, D)


def kernel(x_prompt, x_sample, mix_norm, ffn_norm, final_norm, rg_w_in, rg_conv_w, rg_conv_b, rg_w_a, rg_b_a, rg_w_i, rg_b_i, rg_lambda, rg_w_out, pool_w, pool_scale, ffn_w_gate, ffn_w_up, ffn_w_down, moe_router, moe_w_gate, moe_w_up, moe_w_down):
    layers, final_gain = _prep_weights(
        mix_norm, ffn_norm, final_norm, rg_w_in, rg_conv_w, rg_conv_b, rg_w_a, rg_b_a, rg_w_i, rg_b_i,
        rg_lambda, rg_w_out, pool_w, pool_scale, ffn_w_gate, ffn_w_up, ffn_w_down, moe_router,
        moe_w_gate, moe_w_up, moe_w_down)
    y_prompt = _trunk(x_prompt, layers, final_gain, **_tiles(x_prompt.shape[1]))
    y_sample = _trunk(x_sample, layers, final_gain, **_tiles(x_sample.shape[1]))
    return (y_prompt, y_sample)
```

```python
import functools
import math

import jax
import jax.numpy as jnp
from jax import lax
from jax.experimental import pallas as pl
from jax.experimental.pallas import tpu as pltpu

NORM_EPS = 1e-6
RG_C = 8.0
RG_BLOCK = 256
CONV_WIDTH = 4
CONV_PAD_LEFT = (CONV_WIDTH - 1) // 2
POOL_WINDOWS = (2, 4, 8, 16)
TOP_K = 2
SUBLANES = 8
LANES = 128
HALO = 8
VMEM_LIMIT = 56 * 1024 * 1024
LOG2E = 1.4426950408889634
GELU_C1 = math.sqrt(2.0 / math.pi)
GELU_C3 = 0.044715 * GELU_C1
TINY = 1e-30
SCAN_LC = 64

_BF16 = jnp.bfloat16
_F32 = jnp.float32


def _tiles(seq):
    rg_ts = min(seq, 2048)
    while (2 * seq + 10 * rg_ts) * RG_BLOCK * 4 > VMEM_LIMIT - (3 << 20) and rg_ts > 8 * SCAN_LC:
        rg_ts //= 2
    return dict(tm=1024, tn=512, rg_ts=rg_ts, pool_ts=256)


def _params(*sem):
    return pltpu.CompilerParams(dimension_semantics=sem, vmem_limit_bytes=VMEM_LIMIT)


def _cast_rstd_kernel(x_ref, xb_ref, rs_ref):
    x = x_ref[...]
    xb_ref[...] = x.astype(_BF16)
    rs = lax.rsqrt(jnp.mean(x * x, axis=-1, keepdims=True) + NORM_EPS)
    rs_ref[...] = jnp.broadcast_to(rs, rs_ref.shape)


def _scale_rows(v, rs):
    return v * jnp.concatenate([rs] * (v.shape[1] // LANES), axis=1)


def _cast_rstd(x, *, tm):
    T, K = x.shape
    tm = min(tm, T)
    return pl.pallas_call(
        _cast_rstd_kernel,
        out_shape=(jax.ShapeDtypeStruct((T, K), _BF16), jax.ShapeDtypeStruct((T, LANES), _F32)),
        grid=(T // tm,),
        in_specs=[pl.BlockSpec((tm, K), lambda i: (i, 0))],
        out_specs=(pl.BlockSpec((tm, K), lambda i: (i, 0)), pl.BlockSpec((tm, LANES), lambda i: (i, 0))),
        compiler_params=_params("parallel"),
        name="cast_rstd",
    )(x)


def _sigmoid(x):
    return 1.0 / (1.0 + jnp.exp2(x * (-LOG2E)))


def _gelu_tanh(x):
    inner = x * (GELU_C1 + GELU_C3 * (x * x))
    return (0.5 * x) * (1.0 + jnp.tanh(inner))


def _proj_kernel(x_ref, rs_ref, w_ref, o_ref):
    o_ref[...] = _scale_rows(jnp.dot(x_ref[...], w_ref[...], preferred_element_type=_F32), rs_ref[...])


def _proj_mm(xb, rs, w, *, tm, tn):
    T, K = xb.shape
    N = w.shape[1]
    tm, tn = min(tm, T), min(tn, N)
    return pl.pallas_call(
        _proj_kernel,
        out_shape=jax.ShapeDtypeStruct((T, N), _F32),
        grid=(T // tm, N // tn),
        in_specs=[pl.BlockSpec((tm, K), lambda i, j: (i, 0)),
                  pl.BlockSpec((tm, LANES), lambda i, j: (i, 0)),
                  pl.BlockSpec((K, tn), lambda i, j: (0, j))],
        out_specs=pl.BlockSpec((tm, tn), lambda i, j: (i, j)),
        compiler_params=_params("parallel", "parallel"),
        name="proj_mm",
    )(xb, rs, w)


def _res_kernel(y_ref, w_ref, r_ref, o_ref, ob_ref, rs_ref, *, n_cols):
    j = pl.program_id(1)

    @pl.when(j == 0)
    def _():
        rs_ref[...] = jnp.zeros_like(rs_ref)

    o = r_ref[...] + jnp.dot(y_ref[...], w_ref[...], preferred_element_type=_F32)
    o_ref[...] = o
    ob_ref[...] = o.astype(_BF16)
    o2 = o * o
    part = o2[:, 0:LANES]
    for c in range(1, o.shape[1] // LANES):
        part = part + o2[:, c * LANES:(c + 1) * LANES]
    rs_ref[...] += part

    @pl.when(j == pl.num_programs(1) - 1)
    def _():
        ssq = jnp.sum(rs_ref[...], axis=-1, keepdims=True)
        rs_ref[...] = jnp.broadcast_to(lax.rsqrt(ssq * (1.0 / n_cols) + NORM_EPS), rs_ref.shape)


def _res_mm(y, w, res, *, tm, tn):
    T, K = y.shape
    N = w.shape[1]
    tm, tn = min(tm, T), min(tn, N)
    return pl.pallas_call(
        functools.partial(_res_kernel, n_cols=N),
        out_shape=(jax.ShapeDtypeStruct((T, N), _F32), jax.ShapeDtypeStruct((T, N), _BF16),
                   jax.ShapeDtypeStruct((T, LANES), _F32)),
        grid=(T // tm, N // tn),
        in_specs=[pl.BlockSpec((tm, K), lambda i, j: (i, 0)),
                  pl.BlockSpec((K, tn), lambda i, j: (0, j)),
                  pl.BlockSpec((tm, tn), lambda i, j: (i, j))],
        out_specs=(pl.BlockSpec((tm, tn), lambda i, j: (i, j)),
                   pl.BlockSpec((tm, tn), lambda i, j: (i, j)),
                   pl.BlockSpec((tm, LANES), lambda i, j: (i, 0))),
        compiler_params=_params("parallel", "arbitrary"),
        name="res_mm",
    )(y, w, res)


def _glu_kernel(x_ref, rs_ref, wg_ref, wu_ref, o_ref):
    rs = rs_ref[...]
    g = _scale_rows(jnp.dot(x_ref[...], wg_ref[...], preferred_element_type=_F32), rs)
    u = _scale_rows(jnp.dot(x_ref[...], wu_ref[...], preferred_element_type=_F32), rs)
    o_ref[...] = (g * _sigmoid(g) * u).astype(o_ref.dtype)


def _glu_mm(xb, rs, wg, wu, *, tm, tn):
    T, K = xb.shape
    N = wg.shape[1]
    tm, tn = min(tm, T), min(tn, N)
    return pl.pallas_call(
        _glu_kernel,
        out_shape=jax.ShapeDtypeStruct((T, N), _BF16),
        grid=(T // tm, N // tn),
        in_specs=[pl.BlockSpec((tm, K), lambda i, j: (i, 0)),
                  pl.BlockSpec((tm, LANES), lambda i, j: (i, 0)),
                  pl.BlockSpec((K, tn), lambda i, j: (0, j)),
                  pl.BlockSpec((K, tn), lambda i, j: (0, j))],
        out_specs=pl.BlockSpec((tm, tn), lambda i, j: (i, j)),
        compiler_params=_params("parallel", "parallel"),
        name="glu_mm",
    )(xb, rs, wg, wu)


def _moe_glu_kernel(x_ref, rs_ref, wg_ref, wu_ref, gt_ref, o_ref):
    e = pl.program_id(1)
    rs = rs_ref[...]
    g = _scale_rows(jnp.dot(x_ref[...], wg_ref[...], preferred_element_type=_F32), rs)
    u = _scale_rows(jnp.dot(x_ref[...], wu_ref[...], preferred_element_type=_F32), rs)
    gates = gt_ref[...]
    lane = lax.broadcasted_iota(jnp.int32, gates.shape, 1)
    ge = jnp.sum(jnp.where(lane == e, gates, 0.0), axis=-1, keepdims=True)
    o_ref[...] = (g * _sigmoid(g) * u * ge).astype(o_ref.dtype)


def _moe_glu_mm(xb, rs, wg, wu, gates, *, tm):
    T, K = xb.shape
    E, _, F = wg.shape
    tm = min(tm, T)
    return pl.pallas_call(
        _moe_glu_kernel,
        out_shape=jax.ShapeDtypeStruct((T, E * F), _BF16),
        grid=(T // tm, E),
        in_specs=[pl.BlockSpec((tm, K), lambda i, e: (i, 0)),
                  pl.BlockSpec((tm, LANES), lambda i, e: (i, 0)),
                  pl.BlockSpec((None, K, F), lambda i, e: (e, 0, 0)),
                  pl.BlockSpec((None, K, F), lambda i, e: (e, 0, 0)),
                  pl.BlockSpec((tm, LANES), lambda i, e: (i, 0))],
        out_specs=pl.BlockSpec((tm, F), lambda i, e: (i, e)),
        compiler_params=_params("parallel", "parallel"),
        name="moe_glu_mm",
    )(xb, rs, wg, wu, gates)


def _rg_kernel(rec_ref, prev_ref, next_ref, gate_ref, cw_ref, cb_ref, wa_ref, ba_ref, wi_ref, bi_ref,
               cn_ref, y_ref, hs_ref, us_ref, xi_ref, a_ref, b_ref, hl_ref, pl_ref, carry_ref,
               *, ts, n_chunks, lc, sb):
    p = pl.program_id(2)
    k = pl.program_id(3)
    t = jnp.where(p == 0, k, n_chunks - 1 - k)
    base = pl.multiple_of(t * ts, ts)
    C = rec_ref.shape[-1]
    G = C // LANES
    R = ts // (SUBLANES * lc)
    me = lc + CONV_WIDTH - 1
    nk = sb // SUBLANES
    per_stream = lc // nk
    row = lax.broadcasted_iota(jnp.int32, (SUBLANES, LANES), 0)
    streams = [(r, g) for r in range(R) for g in range(G)]

    @pl.when(k == 0)
    def _():
        carry_ref[...] = jnp.zeros_like(carry_ref)

    ba = ba_ref[...]
    bi = bi_ref[...]
    ch = cn_ref[...] * (0.5 * LOG2E)

    def lanes(g):
        return slice(g * LANES, (g + 1) * LANES)

    def gates(uh, r0):
        ub = uh.astype(_BF16)
        ta = jnp.tanh(jnp.dot(ub, wa_ref[...], preferred_element_type=_F32) + ba)
        ti = jnp.tanh(jnp.dot(ub, wi_ref[...], preferred_element_type=_F32) + bi)
        a = jnp.exp2(ch + ch * ta)
        x = 1.0 - a * a
        b = x * lax.rsqrt(jnp.maximum(x, TINY)) * (1.0 + ti) * uh
        for g in range(G):
            a_ref[g, pl.ds(r0, sb), :] = a[:, lanes(g)]
            b_ref[g, pl.ds(r0, sb), :] = b[:, lanes(g)]

    def piece(g, q):
        r, j = q // SUBLANES, q % SUBLANES
        return (g, pl.ds(r * (lc * SUBLANES) + j, lc, stride=SUBLANES), slice(None))

    def tile(g, r, k8):
        return (g, pl.ds(k8 + r * (lc * SUBLANES), SUBLANES), slice(None))

    def local_scan(reverse):
        def body(i, hp):
            k8 = pl.multiple_of(((lc - 1 - i) if reverse else i) * SUBLANES, SUBLANES)
            out = []
            for n, (r, g) in enumerate(streams):
                at = a_ref[tile(g, r, k8)]
                h = at * hp[2 * n] + b_ref[tile(g, r, k8)]
                pr = at * hp[2 * n + 1]
                hl_ref[tile(g, r, k8)] = h
                pl_ref[tile(g, r, k8)] = pr
                out += [h, pr]
            return tuple(out)

        init = (jnp.zeros((SUBLANES, LANES), _F32), jnp.ones((SUBLANES, LANES), _F32)) * len(streams)
        return lax.fori_loop(0, lc, body, init, unroll=4)

    def sub_chunk_states(hp, reverse):
        cvecs = {}
        edge = 0 if reverse else SUBLANES - 1
        for g in range(G):
            cur = carry_ref[g]
            for r in (range(R - 1, -1, -1) if reverse else range(R)):
                n = streams.index((r, g))
                b, a = hp[2 * n], hp[2 * n + 1]
                for d in (1, 2, 4):
                    shift, valid = (SUBLANES - d, row < SUBLANES - d) if reverse else (d, row >= d)
                    b = b + a * jnp.where(valid, pltpu.roll(b, shift, 0), 0.0)
                    a = a * jnp.where(valid, pltpu.roll(a, shift, 0), 1.0)
                after = b + a * cur
                first = row == (SUBLANES - 1 if reverse else 0)
                cvecs[(r, g)] = jnp.where(first, cur, pltpu.roll(after, SUBLANES - 1 if reverse else 1, 0))
                cur = jnp.broadcast_to(after[edge:edge + 1, :], (SUBLANES, LANES))
            carry_ref[g] = cur
        return cvecs

    def sub_block(s):
        r, h = s // per_stream, s % per_stream
        return (pl.multiple_of((r * lc + h * nk) * SUBLANES, SUBLANES),
                pl.multiple_of((r * me + h * nk) * SUBLANES, SUBLANES))

    @pl.when(p == 0)
    def _():
        for q in range(SUBLANES * R):
            r, j = divmod(q, SUBLANES)
            for g in range(G):
                xi_ref[g, pl.ds((r * me + 1) * SUBLANES + j, lc, stride=SUBLANES), :] = (
                    rec_ref[pl.ds(q * lc, lc), lanes(g)])

        def step_tile(g, r, kk):
            return xi_ref[g, pl.ds((r * me + 1 + kk) * SUBLANES, SUBLANES), :]

        def bcast(v):
            return jnp.broadcast_to(v, (SUBLANES, LANES))

        for g in range(G):
            before = jnp.where(t > 0, prev_ref[HALO - 1:HALO, lanes(g)], 0.0)
            after = jnp.where(t < n_chunks - 1, next_ref[0:2, lanes(g)], 0.0)
            for r in range(R):
                fill = step_tile(g, r - 1, lc - 1)[SUBLANES - 1:SUBLANES, :] if r > 0 else before
                xi_ref[g, pl.ds(r * me * SUBLANES, SUBLANES), :] = jnp.where(
                    row == 0, bcast(fill), pltpu.roll(step_tile(g, r, lc - 1), 1, 0))
                for d in range(2):
                    fill = step_tile(g, r + 1, d)[0:1, :] if r + 1 < R else after[d:d + 1, :]
                    xi_ref[g, pl.ds((r * me + 1 + lc + d) * SUBLANES, SUBLANES), :] = jnp.where(
                        row == SUBLANES - 1, bcast(fill), pltpu.roll(step_tile(g, r, d), SUBLANES - 1, 0))

        cw = cw_ref[...]
        cb = cb_ref[...]

        def gates_body(s, _):
            r0, x0 = sub_block(s)
            parts = []
            for g in range(G):
                win = xi_ref[g, pl.ds(x0, sb + (CONV_WIDTH - 1) * SUBLANES), :]
                uh = cb[:, lanes(g)]
                for tap in range(CONV_WIDTH):
                    uh = uh + cw[tap:tap + 1, lanes(g)] * win[tap * SUBLANES:tap * SUBLANES + sb, :]
                us_ref[g, pl.ds(base + r0, sb), :] = uh
                parts.append(uh)
            gates(jnp.concatenate(parts, axis=1), r0)
            return 0

        lax.fori_loop(0, ts // sb, gates_body, 0, unroll=2)
        cvecs = sub_chunk_states(local_scan(False), False)

        def fix_body(kk, _):
            k8 = pl.multiple_of(kk * SUBLANES, SUBLANES)
            for r, g in streams:
                hs_ref[tile(g, r, base + k8)] = hl_ref[tile(g, r, k8)] + pl_ref[tile(g, r, k8)] * cvecs[(r, g)]
            return 0

        lax.fori_loop(0, lc, fix_body, 0, unroll=4)

    @pl.when(p == 1)
    def _():
        def gates_body(s, _):
            r0, _ = sub_block(s)
            gates(jnp.concatenate([us_ref[g, pl.ds(base + r0, sb), :] for g in range(G)], axis=1), r0)
            return 0

        lax.fori_loop(0, ts // sb, gates_body, 0, unroll=4)
        cvecs = sub_chunk_states(local_scan(True), True)

        def fix_body(kk, _):
            k8 = pl.multiple_of(kk * SUBLANES, SUBLANES)
            for r, g in streams:
                h = hl_ref[tile(g, r, k8)] + pl_ref[tile(g, r, k8)] * cvecs[(r, g)]
                a_ref[tile(g, r, k8)] = h + hs_ref[tile(g, r, base + k8)]
            return 0

        lax.fori_loop(0, lc, fix_body, 0, unroll=4)

        def out_body(s, _):
            r0 = pl.multiple_of(s * sb, sb)
            gate = _gelu_tanh(gate_ref[pl.ds(r0, sb), :])
            for i in range(sb // lc):
                for g in range(G):
                    y_ref[pl.ds(r0 + i * lc, lc), lanes(g)] = (
                        a_ref[piece(g, s * (sb // lc) + i)] * gate[i * lc:(i + 1) * lc, lanes(g)]
                    ).astype(y_ref.dtype)
            return 0

        lax.fori_loop(0, ts // sb, out_body, 0)


def _rg_core(proj, conv_w, conv_b, w_a, b_a, w_i, b_i, cneg, *, ts, lc=SCAN_LC):
    B, S, D2 = proj.shape
    D = D2 // 2
    C = RG_BLOCK
    G = C // LANES
    nc = D // C
    ts = min(ts, S)
    n = S // ts
    lc = min(lc, ts // SUBLANES)
    sb = min(256, lc * SUBLANES)
    assert lc % SUBLANES == 0 and ts % (SUBLANES * lc) == 0 and (lc * SUBLANES) % sb == 0 and sb % lc == 0
    assert (CONV_WIDTH, CONV_PAD_LEFT) == (4, 1)
    tpb = ts // HALO

    def t_of(p, k):
        return jnp.where(p == 0, k, n - 1)

    def t_out(p, k):
        return jnp.where(p == 0, n - 1, n - 1 - k)

    kern = functools.partial(_rg_kernel, ts=ts, n_chunks=n, lc=lc, sb=sb)
    return pl.pallas_call(
        kern,
        out_shape=jax.ShapeDtypeStruct((B, S, D), _BF16),
        grid=(B, nc, 2, n),
        in_specs=[
            pl.BlockSpec((None, ts, C), lambda b, c, p, k: (b, t_of(p, k), nc + c)),
            pl.BlockSpec((None, HALO, C), lambda b, c, p, k: (b, jnp.maximum(t_of(p, k) * tpb - 1, 0), nc + c)),
            pl.BlockSpec((None, HALO, C),
                         lambda b, c, p, k: (b, jnp.minimum((t_of(p, k) + 1) * tpb, S // HALO - 1), nc + c)),
            pl.BlockSpec((None, ts, C), lambda b, c, p, k: (b, t_out(p, k), c)),
            pl.BlockSpec((CONV_WIDTH, C), lambda b, c, p, k: (0, c)),
            pl.BlockSpec((1, C), lambda b, c, p, k: (0, c)),
            pl.BlockSpec((None, None, C, C), lambda b, c, p, k: (p, c, 0, 0)),
            pl.BlockSpec((None, 1, C), lambda b, c, p, k: (p, 0, c)),
            pl.BlockSpec((None, None, C, C), lambda b, c, p, k: (p, c, 0, 0)),
            pl.BlockSpec((None, 1, C), lambda b, c, p, k: (p, 0, c)),
            pl.BlockSpec((None, 1, C), lambda b, c, p, k: (p, 0, c)),
        ],
        out_specs=pl.BlockSpec((None, ts, C), lambda b, c, p, k: (b, t_out(p, k), c)),
        scratch_shapes=[
            pltpu.VMEM((G, S, LANES), _F32),
            pltpu.VMEM((G, S, LANES), _F32),
            pltpu.VMEM((G, (ts // lc // SUBLANES) * (lc + CONV_WIDTH - 1) * SUBLANES, LANES), _F32),
            pltpu.VMEM((G, ts, LANES), _F32),
            pltpu.VMEM((G, ts, LANES), _F32),
            pltpu.VMEM((G, ts, LANES), _F32),
            pltpu.VMEM((G, ts, LANES), _F32),
            pltpu.VMEM((G, SUBLANES, LANES), _F32),
        ],
        compiler_params=_params("parallel", "parallel", "arbitrary", "arbitrary"),
        name="rg_core",
    )(proj, proj, proj, proj, conv_w, conv_b, w_a, b_a, w_i, b_i, cneg)


def _pool_kernel(x_ref, prev_ref, next_ref, pw_ref, ps_ref, rh_ref, rl_ref,
                 o_ref, ob_ref, gt_ref, rs_ref, ext_ref, *, ts, n_chunks, seq, n_experts):
    t = pl.program_id(1)
    D = x_ref.shape[-1]
    G = len(POOL_WINDOWS)
    gs = D // G
    L = ts + 2 * HALO
    assert POOL_WINDOWS[-1] == 2 * HALO and all(w <= HALO for w in POOL_WINDOWS[:-1])

    def scaled(v):
        return v * lax.rsqrt(jnp.mean(v * v, axis=-1, keepdims=True) + NORM_EPS)

    ext_ref[pl.ds(HALO, ts), :] = scaled(x_ref[...])
    ext_ref[pl.ds(0, HALO), :] = jnp.where(t > 0, scaled(prev_ref[...]), 0.0)
    ext_ref[pl.ds(HALO + ts, HALO), :] = jnp.where(t < n_chunks - 1, scaled(next_ref[...]), 0.0)

    def ahead(v, n):
        return pltpu.roll(v, L - n, 0)

    tpos = t * ts + lax.broadcasted_iota(jnp.int32, (ts, 1), 0)
    ssq = jnp.zeros((ts, 1), _F32)
    logits = jnp.zeros((ts, LANES), _F32)
    for gi, win in enumerate(POOL_WINDOWS):
        cols = pl.ds(gi * gs, gs)
        e = ext_ref[:, cols]
        if win == 2:
            s = (pltpu.roll(e, 1, 0) + e)[HALO:HALO + ts, :]
        else:
            f, w = e, 1
            while w < min(win, HALO):
                f = f + ahead(f, w)
                w *= 2
            if win < 2 * HALO:
                s = pltpu.roll(f, win // 2, 0)[HALO:HALO + ts, :]
            else:
                s = f[0:ts, :] + f[HALO:HALO + ts, :]
        half = win // 2
        cnt = jnp.minimum(tpos + (win - half), seq) - jnp.maximum(tpos - half, 0)
        d = s * (1.0 / cnt.astype(_F32)) - e[HALO:HALO + ts, :]
        y = jnp.dot(d.astype(_BF16), pw_ref[gi], preferred_element_type=_F32)
        o = x_ref[:, cols] + y * ps_ref[:, cols]
        o_ref[:, cols] = o
        oh = o.astype(_BF16)
        ob_ref[:, cols] = oh
        ssq = ssq + jnp.sum(o * o, axis=-1, keepdims=True)
        ol = (o - oh.astype(_F32)).astype(_BF16)
        rh = rh_ref[cols, :]
        logits = (logits + jnp.dot(oh, rh, preferred_element_type=_F32)
                  + jnp.dot(oh, rl_ref[cols, :], preferred_element_type=_F32)
                  + jnp.dot(ol, rh, preferred_element_type=_F32))

    rs = lax.rsqrt(ssq * (1.0 / D) + NORM_EPS)
    rs_ref[...] = jnp.broadcast_to(rs, rs_ref.shape)
    logits = logits * rs
    lane = lax.broadcasted_iota(jnp.int32, (ts, LANES), 1).astype(_F32)
    neg = jnp.float32(-jnp.inf)
    logits = jnp.where(lane < n_experts, logits, neg)
    m1 = jnp.max(logits, axis=-1, keepdims=True)
    i1 = jnp.min(jnp.where(logits == m1, lane, float(LANES)), axis=-1, keepdims=True)
    rest = jnp.where(lane == i1, neg, logits)
    m2 = jnp.max(rest, axis=-1, keepdims=True)
    i2 = jnp.min(jnp.where(rest == m2, lane, float(LANES)), axis=-1, keepdims=True)
    e2 = jnp.exp(m2 - m1)
    g1 = 1.0 / (1.0 + e2)
    gt_ref[...] = jnp.where(lane == i1, g1, 0.0) + jnp.where(lane == i2, e2 * g1, 0.0)


def _pool_mix(x, pool_w, pool_scale, r_hi, r_lo, n_experts, *, ts):
    B, S, D = x.shape
    ts = min(ts, S)
    n = S // ts
    tpb = ts // HALO
    G, gs, _ = pool_w.shape
    kern = functools.partial(_pool_kernel, ts=ts, n_chunks=n, seq=S, n_experts=n_experts)
    return pl.pallas_call(
        kern,
        out_shape=(jax.ShapeDtypeStruct((B, S, D), _F32), jax.ShapeDtypeStruct((B, S, D), _BF16),
                   jax.ShapeDtypeStruct((B, S, LANES), _F32), jax.ShapeDtypeStruct((B, S, LANES), _F32)),
        grid=(B, n),
        in_specs=[
            pl.BlockSpec((None, ts, D), lambda b, t: (b, t, 0)),
            pl.BlockSpec((None, HALO, D), lambda b, t: (b, jnp.maximum(t * tpb - 1, 0), 0)),
            pl.BlockSpec((None, HALO, D), lambda b, t: (b, jnp.minimum((t + 1) * tpb, S // HALO - 1), 0)),
            pl.BlockSpec((G, gs, gs), lambda b, t: (0, 0, 0)),
            pl.BlockSpec((1, D), lambda b, t: (0, 0)),
            pl.BlockSpec((D, LANES), lambda b, t: (0, 0)),
            pl.BlockSpec((D, LANES), lambda b, t: (0, 0)),
        ],
        out_specs=(pl.BlockSpec((None, ts, D), lambda b, t: (b, t, 0)),
                   pl.BlockSpec((None, ts, D), lambda b, t: (b, t, 0)),
                   pl.BlockSpec((None, ts, LANES), lambda b, t: (b, t, 0)),
                   pl.BlockSpec((None, ts, LANES), lambda b, t: (b, t, 0))),
        scratch_shapes=[pltpu.VMEM((ts + 2 * HALO, D), _F32)],
        compiler_params=_params("parallel", "parallel"),
        name="pool_mix",
    )(x, x, x, pool_w, pool_scale, r_hi, r_lo)


def _res_norm_kernel(y_ref, w_ref, r_ref, g_ref, o_ref, *, tn, rb):
    j = pl.program_id(1)
    c0 = pl.multiple_of(j * tn, tn)
    o_ref[:, pl.ds(c0, tn)] = r_ref[...] + jnp.dot(y_ref[...], w_ref[...], preferred_element_type=_F32)

    @pl.when(j == pl.num_programs(1) - 1)
    def _():
        def body(s, _):
            r0 = pl.multiple_of(s * rb, rb)
            x = o_ref[pl.ds(r0, rb), :]
            o_ref[pl.ds(r0, rb), :] = (x * lax.rsqrt(jnp.mean(x * x, axis=-1, keepdims=True) + NORM_EPS)
                                       * g_ref[...])
            return 0

        lax.fori_loop(0, o_ref.shape[0] // rb, body, 0)


def _res_norm_mm(y, w, res, gain, *, tm, tn):
    T, K = y.shape
    N = w.shape[1]
    tm, tn = min(tm, T), min(tn, N)
    rb = min(64, tm)
    return pl.pallas_call(
        functools.partial(_res_norm_kernel, tn=tn, rb=rb),
        out_shape=jax.ShapeDtypeStruct((T, N), _F32),
        grid=(T // tm, N // tn),
        in_specs=[pl.BlockSpec((tm, K), lambda i, j: (i, 0)),
                  pl.BlockSpec((K, tn), lambda i, j: (0, j)),
                  pl.BlockSpec((tm, tn), lambda i, j: (i, j)),
                  pl.BlockSpec((1, N), lambda i, j: (0, 0))],
        out_specs=pl.BlockSpec((tm, N), lambda i, j: (i, 0)),
        compiler_params=_params("parallel", "arbitrary"),
        name="res_norm_mm",
    )(y, w, res, gain)


def _fold(gain, w):
    return (gain[:, None] * w).astype(_BF16)


def _prep_weights(mix_norm, ffn_norm, final_norm, rg_w_in, rg_conv_w, rg_conv_b, rg_w_a, rg_b_a, rg_w_i,
                  rg_b_i, rg_lambda, rg_w_out, pool_w, pool_scale, ffn_w_gate, ffn_w_up, ffn_w_down,
                  moe_router, moe_w_gate, moe_w_up, moe_w_down):
    depth, D = mix_norm.shape
    layers = []
    for i in range(depth):
        j = i // 2
        lw = {}
        if i % 2 == 0:
            lw["w_in"] = _fold(mix_norm[i], rg_w_in[j])
            lw["conv_w"] = 0.5 * rg_conv_w[j]
            lw["conv_b"] = 0.5 * rg_conv_b[j][None, :]
            lw["w_a"] = rg_w_a[j].astype(_BF16)
            lw["b_a"] = 0.5 * rg_b_a[j][:, None, :]
            lw["w_i"] = rg_w_i[j].astype(_BF16)
            lw["b_i"] = 0.5 * rg_b_i[j][:, None, :]
            lw["cneg"] = (-RG_C * jax.nn.softplus(-rg_lambda[j]))[:, None, :]
            lw["w_out"] = rg_w_out[j].astype(_BF16)
            lw["w_gate"] = _fold(ffn_norm[i], ffn_w_gate[j])
            lw["w_up"] = _fold(ffn_norm[i], ffn_w_up[j])
            lw["w_down"] = ffn_w_down[j].astype(_BF16)
        else:
            E = moe_router.shape[-1]
            G, gs = pool_w.shape[1:3]
            lw["pool_w"] = (mix_norm[i].reshape(G, gs, 1) * pool_w[j]).astype(_BF16)
            lw["pool_scale"] = pool_scale[j][None, :]
            router = jnp.pad(ffn_norm[i][:, None] * moe_router[j], ((0, 0), (0, LANES - E)))
            lw["r_hi"] = router.astype(_BF16)
            lw["r_lo"] = (router - lw["r_hi"].astype(_F32)).astype(_BF16)
            lw["n_experts"] = E
            lw["w_gate"] = _fold(ffn_norm[i], moe_w_gate[j])
            lw["w_up"] = _fold(ffn_norm[i], moe_w_up[j])
            lw["w_down"] = moe_w_down[j].reshape(-1, D).astype(_BF16)
        layers.append(lw)
    return layers, final_norm[None, :]


def _trunk(x, layers, final_gain, *, tm, tn, rg_ts, pool_ts):
    B, S, D = x.shape
    T = B * S
    xf = x.reshape(T, D)
    xb, rs = _cast_rstd(xf, tm=tm // 4)
    for i, lw in enumerate(layers):
        if i % 2 == 0:
            proj = _proj_mm(xb, rs, lw["w_in"], tm=tm, tn=2 * tn)
            y = _rg_core(proj.reshape(B, S, 2 * D), lw["conv_w"], lw["conv_b"], lw["w_a"], lw["b_a"],
                         lw["w_i"], lw["b_i"], lw["cneg"], ts=rg_ts)
            xf, xb, rs = _res_mm(y.reshape(T, D), lw["w_out"], xf, tm=tm, tn=tn)
            h = _glu_mm(xb, rs, lw["w_gate"], lw["w_up"], tm=tm, tn=tn)
        else:
            o, ob, gates, rs = _pool_mix(xf.reshape(B, S, D), lw["pool_w"], lw["pool_scale"],
                                         lw["r_hi"], lw["r_lo"], lw["n_experts"], ts=pool_ts)
            xf, xb, rs = o.reshape(T, D), ob.reshape(T, D), rs.reshape(T, LANES)
            h = _moe_glu_mm(xb, rs, lw["w_gate"], lw["w_up"], gates.reshape(T, LANES), tm=tm)
        if i + 1 < len(layers):
            xf, xb, rs = _res_mm(h, lw["w_down"], xf, tm=tm, tn=tn)
    return _res_norm_mm(h, lw["w_down"], xf, final_gain, tm=tm // 2, tn=tn).reshape(B, S, D)


def kernel(x_prompt, x_sample, mix_norm, ffn_norm, final_norm, rg_w_in, rg_conv_w, rg_conv_b, rg_w_a, rg_b_a, rg_w_i, rg_b_i, rg_lambda, rg_w_out, pool_w, pool_scale, ffn_w_gate, ffn_w_up, ffn_w_down, moe_router, moe_w_gate, moe_w_up, moe_w_down):
    layers, final_gain = _prep_weights(
        mix_norm, ffn_norm, final_norm, rg_w_in, rg_conv_w, rg_conv_b, rg_w_a, rg_b_a, rg_w_i, rg_b_i,
        rg_lambda, rg_w_out, pool_w, pool_scale, ffn_w_gate, ffn_w_up, ffn_w_down, moe_router,
        moe_w_gate, moe_w_up, moe_w_down)
    y_prompt = _trunk(x_prompt, layers, final_gain, **_tiles(x_prompt.shape[1]))
    y_sample = _trunk(x_sample, layers, final_gain, **_tiles(x_sample.shape[1]))
    return (y_prompt, y_sample)
```

```python
import functools
import math

import jax
import jax.numpy as jnp
from jax import lax
from jax.experimental import pallas as pl
from jax.experimental.pallas import tpu as pltpu

NORM_EPS = 1e-6
RG_C = 8.0
RG_BLOCK = 256
CONV_WIDTH = 4
CONV_PAD_LEFT = (CONV_WIDTH - 1) // 2
POOL_WINDOWS = (2, 4, 8, 16)
TOP_K = 2
SUBLANES = 8
LANES = 128
HALO = 8
VMEM_LIMIT = 56 * 1024 * 1024
LOG2E = 1.4426950408889634
GELU_C1 = math.sqrt(2.0 / math.pi)
GELU_C3 = 0.044715 * GELU_C1
TINY = 1e-30
SCAN_LC = 64

_BF16 = jnp.bfloat16
_F32 = jnp.float32


def _tiles(seq):
    rg_ts = min(seq, 2048)
    while (2 * seq + 10 * rg_ts) * RG_BLOCK * 4 > VMEM_LIMIT - (3 << 20) and rg_ts > 8 * SCAN_LC:
        rg_ts //= 2
    return dict(tm=1024, tn=512, rg_ts=rg_ts, pool_ts=256)


def _params(*sem):
    return pltpu.CompilerParams(dimension_semantics=sem, vmem_limit_bytes=VMEM_LIMIT)


def _cast_rstd_kernel(x_ref, xb_ref, rs_ref):
    x = x_ref[...]
    xb_ref[...] = x.astype(_BF16)
    rs = lax.rsqrt(jnp.mean(x * x, axis=-1, keepdims=True) + NORM_EPS)
    rs_ref[...] = jnp.broadcast_to(rs, rs_ref.shape)


def _scale_rows(v, rs):
    return v * jnp.concatenate([rs] * (v.shape[1] // LANES), axis=1)


def _cast_rstd(x, *, tm):
    T, K = x.shape
    tm = min(tm, T)
    return pl.pallas_call(
        _cast_rstd_kernel,
        out_shape=(jax.ShapeDtypeStruct((T, K), _BF16), jax.ShapeDtypeStruct((T, LANES), _F32)),
        grid=(T // tm,),
        in_specs=[pl.BlockSpec((tm, K), lambda i: (i, 0))],
        out_specs=(pl.BlockSpec((tm, K), lambda i: (i, 0)), pl.BlockSpec((tm, LANES), lambda i: (i, 0))),
        compiler_params=_params("parallel"),
        name="cast_rstd",
    )(x)


def _sigmoid(x):
    return 1.0 / (1.0 + jnp.exp2(x * (-LOG2E)))


def _gelu_tanh(x):
    inner = x * (GELU_C1 + GELU_C3 * (x * x))
    return (0.5 * x) * (1.0 + jnp.tanh(inner))


def _proj_kernel(x_ref, rs_ref, w_ref, o_ref):
    o_ref[...] = _scale_rows(jnp.dot(x_ref[...], w_ref[...], preferred_element_type=_F32), rs_ref[...])


def _proj_mm(xb, rs, w, *, tm, tn):
    T, K = xb.shape
    N = w.shape[1]
    tm, tn = min(tm, T), min(tn, N)
    return pl.pallas_call(
        _proj_kernel,
        out_shape=jax.ShapeDtypeStruct((T, N), _F32),
        grid=(T // tm, N // tn),
        in_specs=[pl.BlockSpec((tm, K), lambda i, j: (i, 0)),
                  pl.BlockSpec((tm, LANES), lambda i, j: (i, 0)),
                  pl.BlockSpec((K, tn), lambda i, j: (0, j))],
        out_specs=pl.BlockSpec((tm, tn), lambda i, j: (i, j)),
        compiler_params=_params("parallel", "parallel"),
        name="proj_mm",
    )(xb, rs, w)


def _res_kernel(y_ref, w_ref, r_ref, o_ref, ob_ref, rs_ref, *, n_cols):
    j = pl.program_id(1)

    @pl.when(j == 0)
    def _():
        rs_ref[...] = jnp.zeros_like(rs_ref)

    o = r_ref[...] + jnp.dot(y_ref[...], w_ref[...], preferred_element_type=_F32)
    o_ref[...] = o
    ob_ref[...] = o.astype(_BF16)
    o2 = o * o
    part = o2[:, 0:LANES]
    for c in range(1, o.shape[1] // LANES):
        part = part + o2[:, c * LANES:(c + 1) * LANES]
    rs_ref[...] += part

    @pl.when(j == pl.num_programs(1) - 1)
    def _():
        ssq = jnp.sum(rs_ref[...], axis=-1, keepdims=True)
        rs_ref[...] = jnp.broadcast_to(lax.rsqrt(ssq * (1.0 / n_cols) + NORM_EPS), rs_ref.shape)


def _res_mm(y, w, res, *, tm, tn):
    T, K = y.shape
    N = w.shape[1]
    tm, tn = min(tm, T), min(tn, N)
    return pl.pallas_call(
        functools.partial(_res_kernel, n_cols=N),
        out_shape=(jax.ShapeDtypeStruct((T, N), _F32), jax.ShapeDtypeStruct((T, N), _BF16),
                   jax.ShapeDtypeStruct((T, LANES), _F32)),
        grid=(T // tm, N // tn),
        in_specs=[pl.BlockSpec((tm, K), lambda i, j: (i, 0)),
                  pl.BlockSpec((K, tn), lambda i, j: (0, j)),
                  pl.BlockSpec((tm, tn), lambda i, j: (i, j))],
        out_specs=(pl.BlockSpec((tm, tn), lambda i, j: (i, j)),
                   pl.BlockSpec((tm, tn), lambda i, j: (i, j)),
                   pl.BlockSpec((tm, LANES), lambda i, j: (i, 0))),
        compiler_params=_params("parallel", "arbitrary"),
        name="res_mm",
    )(y, w, res)


def _glu_kernel(x_ref, rs_ref, wg_ref, wu_ref, o_ref):
    rs = rs_ref[...]
    g = _scale_rows(jnp.dot(x_ref[...], wg_ref[...], preferred_element_type=_F32), rs)
    u = _scale_rows(jnp.dot(x_ref[...], wu_ref[...], preferred_element_type=_F32), rs)
    o_ref[...] = (g * _sigmoid(g) * u).astype(o_ref.dtype)


def _glu_mm(xb, rs, wg, wu, *, tm, tn):
    T, K = xb.shape
    N = wg.shape[1]
    tm, tn = min(tm, T), min(tn, N)
    return pl.pallas_call(
        _glu_kernel,
        out_shape=jax.ShapeDtypeStruct((T, N), _BF16),
        grid=(T // tm, N // tn),
        in_specs=[pl.BlockSpec((tm, K), lambda i, j: (i, 0)),
                  pl.BlockSpec((tm, LANES), lambda i, j: (i, 0)),
                  pl.BlockSpec((K, tn), lambda i, j: (0, j)),
                  pl.BlockSpec((K, tn), lambda i, j: (0, j))],
        out_specs=pl.BlockSpec((tm, tn), lambda i, j: (i, j)),
        compiler_params=_params("parallel", "parallel"),
        name="glu_mm",
    )(xb, rs, wg, wu)


def _moe_glu_kernel(x_ref, rs_ref, wg_ref, wu_ref, gt_ref, o_ref):
    e = pl.program_id(1)
    rs = rs_ref[...]
    g = _scale_rows(jnp.dot(x_ref[...], wg_ref[...], preferred_element_type=_F32), rs)
    u = _scale_rows(jnp.dot(x_ref[...], wu_ref[...], preferred_element_type=_F32), rs)
    gates = gt_ref[...]
    lane = lax.broadcasted_iota(jnp.int32, gates.shape, 1)
    ge = jnp.sum(jnp.where(lane == e, gates, 0.0), axis=-1, keepdims=True)
    o_ref[...] = (g * _sigmoid(g) * u * ge).astype(o_ref.dtype)


def _moe_glu_mm(xb, rs, wg, wu, gates, *, tm):
    T, K = xb.shape
    E, _, F = wg.shape
    tm = min(tm, T)
    return pl.pallas_call(
        _moe_glu_kernel,
        out_shape=jax.ShapeDtypeStruct((T, E * F), _BF16),
        grid=(T // tm, E),
        in_specs=[pl.BlockSpec((tm, K), lambda i, e: (i, 0)),
                  pl.BlockSpec((tm, LANES), lambda i, e: (i, 0)),
                  pl.BlockSpec((None, K, F), lambda i, e: (e, 0, 0)),
                  pl.BlockSpec((None, K, F), lambda i, e: (e, 0, 0)),
                  pl.BlockSpec((tm, LANES), lambda i, e: (i, 0))],
        out_specs=pl.BlockSpec((tm, F), lambda i, e: (i, e)),
        compiler_params=_params("parallel", "parallel"),
        name="moe_glu_mm",
    )(xb, rs, wg, wu, gates)


def _rg_kernel(rec_ref, prev_ref, next_ref, gate_ref, cw_ref, cb_ref, wa_ref, ba_ref, wi_ref, bi_ref,
               cn_ref, y_ref, hs_ref, us_ref, xi_ref, a_ref, b_ref, hl_ref, pl_ref, carry_ref,
               *, ts, n_chunks, lc, sb):
    p = pl.program_id(2)
    k = pl.program_id(3)
    t = jnp.where(p == 0, k, n_chunks - 1 - k)
    base = pl.multiple_of(t * ts, ts)
    C = rec_ref.shape[-1]
    G = C // LANES
    R = ts // (SUBLANES * lc)
    me = lc + CONV_WIDTH - 1
    nk = sb // SUBLANES
    per_stream = lc // nk
    row = lax.broadcasted_iota(jnp.int32, (SUBLANES, LANES), 0)
    streams = [(r, g) for r in range(R) for g in range(G)]

    @pl.when(k == 0)
    def _():
        carry_ref[...] = jnp.zeros_like(carry_ref)

    ba = ba_ref[...]
    bi = bi_ref[...]
    ch = cn_ref[...] * (0.5 * LOG2E)

    def lanes(g):
        return slice(g * LANES, (g + 1) * LANES)

    def gates(uh, r0):
        ub = uh.astype(_BF16)
        ta = jnp.tanh(jnp.dot(ub, wa_ref[...], preferred_element_type=_F32) + ba)
        ti = jnp.tanh(jnp.dot(ub, wi_ref[...], preferred_element_type=_F32) + bi)
        a = jnp.exp2(ch + ch * ta)
        x = 1.0 - a * a
        b = x * lax.rsqrt(jnp.maximum(x, TINY)) * (1.0 + ti) * uh
        for g in range(G):
            a_ref[g, pl.ds(r0, sb), :] = a[:, lanes(g)]
            b_ref[g, pl.ds(r0, sb), :] = b[:, lanes(g)]

    def piece(g, q):
        r, j = q // SUBLANES, q % SUBLANES
        return (g, pl.ds(r * (lc * SUBLANES) + j, lc, stride=SUBLANES), slice(None))

    def tile(g, r, k8):
        return (g, pl.ds(k8 + r * (lc * SUBLANES), SUBLANES), slice(None))

    def local_scan(reverse):
        def body(i, hp):
            k8 = pl.multiple_of(((lc - 1 - i) if reverse else i) * SUBLANES, SUBLANES)
            out = []
            for n, (r, g) in enumerate(streams):
                at = a_ref[tile(g, r, k8)]
                h = at * hp[2 * n] + b_ref[tile(g, r, k8)]
                pr = at * hp[2 * n + 1]
                hl_ref[tile(g, r, k8)] = h
                pl_ref[tile(g, r, k8)] = pr
                out += [h, pr]
            return tuple(out)

        init = (jnp.zeros((SUBLANES, LANES), _F32), jnp.ones((SUBLANES, LANES), _F32)) * len(streams)
        return lax.fori_loop(0, lc, body, init, unroll=4)

    def sub_chunk_states(hp, reverse):
        cvecs = {}
        edge = 0 if reverse else SUBLANES - 1
        for g in range(G):
            cur = carry_ref[g]
            for r in (range(R - 1, -1, -1) if reverse else range(R)):
                n = streams.index((r, g))
                b, a = hp[2 * n], hp[2 * n + 1]
                for d in (1, 2, 4):
                    shift, valid = (SUBLANES - d, row < SUBLANES - d) if reverse else (d, row >= d)
                    b = b + a * jnp.where(valid, pltpu.roll(b, shift, 0), 0.0)
                    a = a * jnp.where(valid, pltpu.roll(a, shift, 0), 1.0)
                after = b + a * cur
                first = row == (SUBLANES - 1 if reverse else 0)
                cvecs[(r, g)] = jnp.where(first, cur, pltpu.roll(after, SUBLANES - 1 if reverse else 1, 0))
                cur = jnp.broadcast_to(after[edge:edge + 1, :], (SUBLANES, LANES))
            carry_ref[g] = cur
        return cvecs

    def sub_block(s):
        r, h = s // per_stream, s % per_stream
        return (pl.multiple_of((r * lc + h * nk) * SUBLANES, SUBLANES),
                pl.multiple_of((r * me + h * nk) * SUBLANES, SUBLANES))

    @pl.when(p == 0)
    def _():
        for q in range(SUBLANES * R):
            r, j = divmod(q, SUBLANES)
            for g in range(G):
                xi_ref[g, pl.ds((r * me + 1) * SUBLANES + j, lc, stride=SUBLANES), :] = (
                    rec_ref[pl.ds(q * lc, lc), lanes(g)])

        def step_tile(g, r, kk):
            return xi_ref[g, pl.ds((r * me + 1 + kk) * SUBLANES, SUBLANES), :]

        def bcast(v):
            return jnp.broadcast_to(v, (SUBLANES, LANES))

        for g in range(G):
            before = jnp.where(t > 0, prev_ref[HALO - 1:HALO, lanes(g)], 0.0)
            after = jnp.where(t < n_chunks - 1, next_ref[0:2, lanes(g)], 0.0)
            for r in range(R):
                fill = step_tile(g, r - 1, lc - 1)[SUBLANES - 1:SUBLANES, :] if r > 0 else before
                xi_ref[g, pl.ds(r * me * SUBLANES, SUBLANES), :] = jnp.where(
                    row == 0, bcast(fill), pltpu.roll(step_tile(g, r, lc - 1), 1, 0))
                for d in range(2):
                    fill = step_tile(g, r + 1, d)[0:1, :] if r + 1 < R else after[d:d + 1, :]
                    xi_ref[g, pl.ds((r * me + 1 + lc + d) * SUBLANES, SUBLANES), :] = jnp.where(
                        row == SUBLANES - 1, bcast(fill), pltpu.roll(step_tile(g, r, d), SUBLANES - 1, 0))

        cw = cw_ref[...]
        cb = cb_ref[...]

        def gates_body(s, _):
            r0, x0 = sub_block(s)
            parts = []
            for g in range(G):
                win = xi_ref[g, pl.ds(x0, sb + (CONV_WIDTH - 1) * SUBLANES), :]
                uh = cb[:, lanes(g)]
                for tap in range(CONV_WIDTH):
                    uh = uh + cw[tap:tap + 1, lanes(g)] * win[tap * SUBLANES:tap * SUBLANES + sb, :]
                us_ref[g, pl.ds(base + r0, sb), :] = uh
                parts.append(uh)
            gates(jnp.concatenate(parts, axis=1), r0)
            return 0

        lax.fori_loop(0, ts // sb, gates_body, 0, unroll=2)
        cvecs = sub_chunk_states(local_scan(False), False)

        def fix_body(kk, _):
            k8 = pl.multiple_of(kk * SUBLANES, SUBLANES)
            for r, g in streams:
                hs_ref[tile(g, r, base + k8)] = hl_ref[tile(g, r, k8)] + pl_ref[tile(g, r, k8)] * cvecs[(r, g)]
            return 0

        lax.fori_loop(0, lc, fix_body, 0, unroll=4)

    @pl.when(p == 1)
    def _():
        def gates_body(s, _):
            r0, _ = sub_block(s)
            gates(jnp.concatenate([us_ref[g, pl.ds(base + r0, sb), :] for g in range(G)], axis=1), r0)
            return 0

        lax.fori_loop(0, ts // sb, gates_body, 0, unroll=4)
        cvecs = sub_chunk_states(local_scan(True), True)

        def fix_body(kk, _):
            k8 = pl.multiple_of(kk * SUBLANES, SUBLANES)
            for r, g in streams:
                h = hl_ref[tile(g, r, k8)] + pl_ref[tile(g, r, k8)] * cvecs[(r, g)]
                a_ref[tile(g, r, k8)] = h + hs_ref[tile(g, r, base + k8)]
            return 0

        lax.fori_loop(0, lc, fix_body, 0, unroll=4)

        def out_body(s, _):
            r0 = pl.multiple_of(s * sb, sb)
            gate = _gelu_tanh(gate_ref[pl.ds(r0, sb), :])
            for i in range(sb // lc):
                for g in range(G):
                    y_ref[pl.ds(r0 + i * lc, lc), lanes(g)] = (
                        a_ref[piece(g, s * (sb // lc) + i)] * gate[i * lc:(i + 1) * lc, lanes(g)]
                    ).astype(y_ref.dtype)
            return 0

        lax.fori_loop(0, ts // sb, out_body, 0)


def _rg_core(proj, conv_w, conv_b, w_a, b_a, w_i, b_i, cneg, *, ts, lc=SCAN_LC):
    B, S, D2 = proj.shape
    D = D2 // 2
    C = RG_BLOCK
    G = C // LANES
    nc = D // C
    ts = min(ts, S)
    n = S // ts
    lc = min(lc, ts // SUBLANES)
    sb = min(256, lc * SUBLANES)
    assert lc % SUBLANES == 0 and ts % (SUBLANES * lc) == 0 and (lc * SUBLANES) % sb == 0 and sb % lc == 0
    assert (CONV_WIDTH, CONV_PAD_LEFT) == (4, 1)
    tpb = ts // HALO

    def t_of(p, k):
        return jnp.where(p == 0, k, n - 1)

    def t_out(p, k):
        return jnp.where(p == 0, n - 1, n - 1 - k)

    kern = functools.partial(_rg_kernel, ts=ts, n_chunks=n, lc=lc, sb=sb)
    return pl.pallas_call(
        kern,
        out_shape=jax.ShapeDtypeStruct((B, S, D), _BF16),
        grid=(B, nc, 2, n),
        in_specs=[
            pl.BlockSpec((None, ts, C), lambda b, c, p, k: (b, t_of(p, k), nc + c)),
            pl.BlockSpec((None, HALO, C), lambda b, c, p, k: (b, jnp.maximum(t_of(p, k) * tpb - 1, 0), nc + c)),
            pl.BlockSpec((None, HALO, C),
                         lambda b, c, p, k: (b, jnp.minimum((t_of(p, k) + 1) * tpb, S // HALO - 1), nc + c)),
            pl.BlockSpec((None, ts, C), lambda b, c, p, k: (b, t_out(p, k), c)),
            pl.BlockSpec((CONV_WIDTH, C), lambda b, c, p, k: (0, c)),
            pl.BlockSpec((1, C), lambda b, c, p, k: (0, c)),
            pl.BlockSpec((None, None, C, C), lambda b, c, p, k: (p, c, 0, 0)),
            pl.BlockSpec((None, 1, C), lambda b, c, p, k: (p, 0, c)),
            pl.BlockSpec((None, None, C, C), lambda b, c, p, k: (p, c, 0, 0)),
            pl.BlockSpec((None, 1, C), lambda b, c, p, k: (p, 0, c)),
            pl.BlockSpec((None, 1, C), lambda b, c, p, k: (p, 0, c)),
        ],
        out_specs=pl.BlockSpec((None, ts, C), lambda b, c, p, k: (b, t_out(p, k), c)),
        scratch_shapes=[
            pltpu.VMEM((G, S, LANES), _F32),
            pltpu.VMEM((G, S, LANES), _F32),
            pltpu.VMEM((G, (ts // lc // SUBLANES) * (lc + CONV_WIDTH - 1) * SUBLANES, LANES), _F32),
            pltpu.VMEM((G, ts, LANES), _F32),
            pltpu.VMEM((G, ts, LANES), _F32),
            pltpu.VMEM((G, ts, LANES), _F32),
            pltpu.VMEM((G, ts, LANES), _F32),
            pltpu.VMEM((G, SUBLANES, LANES), _F32),
        ],
        compiler_params=_params("parallel", "parallel", "arbitrary", "arbitrary"),
        name="rg_core",
    )(proj, proj, proj, proj, conv_w, conv_b, w_a, b_a, w_i, b_i, cneg)


def _pool_kernel(x_ref, prev_ref, next_ref, pw_ref, ps_ref, rh_ref, rl_ref,
                 o_ref, ob_ref, gt_ref, rs_ref, ext_ref, *, ts, n_chunks, seq, n_experts):
    t = pl.program_id(1)
    D = x_ref.shape[-1]
    G = len(POOL_WINDOWS)
    gs = D // G
    L = ts + 2 * HALO
    assert POOL_WINDOWS[-1] == 2 * HALO and all(w <= HALO for w in POOL_WINDOWS[:-1])

    def scaled(v):
        return v * lax.rsqrt(jnp.mean(v * v, axis=-1, keepdims=True) + NORM_EPS)

    ext_ref[pl.ds(HALO, ts), :] = scaled(x_ref[...])
    ext_ref[pl.ds(0, HALO), :] = jnp.where(t > 0, scaled(prev_ref[...]), 0.0)
    ext_ref[pl.ds(HALO + ts, HALO), :] = jnp.where(t < n_chunks - 1, scaled(next_ref[...]), 0.0)

    def ahead(v, n):
        return pltpu.roll(v, L - n, 0)

    tpos = t * ts + lax.broadcasted_iota(jnp.int32, (ts, 1), 0)
    ssq = jnp.zeros((ts, 1), _F32)
    logits = jnp.zeros((ts, LANES), _F32)
    for gi, win in enumerate(POOL_WINDOWS):
        cols = pl.ds(gi * gs, gs)
        e = ext_ref[:, cols]
        if win == 2:
            s = (pltpu.roll(e, 1, 0) + e)[HALO:HALO + ts, :]
        else:
            f, w = e, 1
            while w < min(win, HALO):
                f = f + ahead(f, w)
                w *= 2
            if win < 2 * HALO:
                s = pltpu.roll(f, win // 2, 0)[HALO:HALO + ts, :]
            else:
                s = f[0:ts, :] + f[HALO:HALO + ts, :]
        half = win // 2
        cnt = jnp.minimum(tpos + (win - half), seq) - jnp.maximum(tpos - half, 0)
        d = s * (1.0 / cnt.astype(_F32)) - e[HALO:HALO + ts, :]
        y = jnp.dot(d.astype(_BF16), pw_ref[gi], preferred_element_type=_F32)
        o = x_ref[:, cols] + y * ps_ref[:, cols]
        o_ref[:, cols] = o
        oh = o.astype(_BF16)
        ob_ref[:, cols] = oh
        ssq = ssq + jnp.sum(o * o, axis=-1, keepdims=True)
        ol = (o - oh.astype(_F32)).astype(_BF16)
        rh = rh_ref[cols, :]
        logits = (logits + jnp.dot(oh, rh, preferred_element_type=_F32)
                  + jnp.dot(oh, rl_ref[cols, :], preferred_element_type=_F32)
                  + jnp.dot(ol, rh, preferred_element_type=_F32))

    rs = lax.rsqrt(ssq * (1.0 / D) + NORM_EPS)
    rs_ref[...] = jnp.broadcast_to(rs, rs_ref.shape)
    logits = logits * rs
    lane = lax.broadcasted_iota(jnp.int32, (ts, LANES), 1).astype(_F32)
    neg = jnp.float32(-jnp.inf)
    logits = jnp.where(lane < n_experts, logits, neg)
    m1 = jnp.max(logits, axis=-1, keepdims=True)
    i1 = jnp.min(jnp.where(logits == m1, lane, float(LANES)), axis=-1, keepdims=True)
    rest = jnp.where(lane == i1, neg, logits)
    m2 = jnp.max(rest, axis=-1, keepdims=True)
    i2 = jnp.min(jnp.where(rest == m2, lane, float(LANES)), axis=-1, keepdims=True)
    e2 = jnp.exp(m2 - m1)
    g1 = 1.0 / (1.0 + e2)
    gt_ref[...] = jnp.where(lane == i1, g1, 0.0) + jnp.where(lane == i2, e2 * g1, 0.0)


def _pool_mix(x, pool_w, pool_scale, r_hi, r_lo, n_experts, *, ts):
    B, S, D = x.shape
    ts = min(ts, S)
    n = S // ts
    tpb = ts // HALO
    G, gs, _ = pool_w.shape
    kern = functools.partial(_pool_kernel, ts=ts, n_chunks=n, seq=S, n_experts=n_experts)
    return pl.pallas_call(
        kern,
        out_shape=(jax.ShapeDtypeStruct((B, S, D), _F32), jax.ShapeDtypeStruct((B, S, D), _BF16),
                   jax.ShapeDtypeStruct((B, S, LANES), _F32), jax.ShapeDtypeStruct((B, S, LANES), _F32)),
        grid=(B, n),
        in_specs=[
            pl.BlockSpec((None, ts, D), lambda b, t: (b, t, 0)),
            pl.BlockSpec((None, HALO, D), lambda b, t: (b, jnp.maximum(t * tpb - 1, 0), 0)),
            pl.BlockSpec((None, HALO, D), lambda b, t: (b, jnp.minimum((t + 1) * tpb, S // HALO - 1), 0)),
            pl.BlockSpec((G, gs, gs), lambda b, t: (0, 0, 0)),
            pl.BlockSpec((1, D), lambda b, t: (0, 0)),
            pl.BlockSpec((D, LANES), lambda b, t: (0, 0)),
            pl.BlockSpec((D, LANES), lambda b, t: (0, 0)),
        ],
        out_specs=(pl.BlockSpec((None, ts, D), lambda b, t: (b, t, 0)),
                   pl.BlockSpec((None, ts, D), lambda b, t: (b, t, 0)),
                   pl.BlockSpec((None, ts, LANES), lambda b, t: (b, t, 0)),
                   pl.BlockSpec((None, ts, LANES), lambda b, t: (b, t, 0))),
        scratch_shapes=[pltpu.VMEM((ts + 2 * HALO, D), _F32)],
        compiler_params=_params("parallel", "parallel"),
        name="pool_mix",
    )(x, x, x, pool_w, pool_scale, r_hi, r_lo)


def _res_norm_kernel(y_ref, w_ref, r_ref, g_ref, o_ref, *, tn, rb):
    j = pl.program_id(1)
    c0 = pl.multiple_of(j * tn, tn)
    o_ref[:, pl.ds(c0, tn)] = r_ref[...] + jnp.dot(y_ref[...], w_ref[...], preferred_element_type=_F32)

    @pl.when(j == pl.num_programs(1) - 1)
    def _():
        def body(s, _):
            r0 = pl.multiple_of(s * rb, rb)
            x = o_ref[pl.ds(r0, rb), :]
            o_ref[pl.ds(r0, rb), :] = (x * lax.rsqrt(jnp.mean(x * x, axis=-1, keepdims=True) + NORM_EPS)
                                       * g_ref[...])
            return 0

        lax.fori_loop(0, o_ref.shape[0] // rb, body, 0)


def _res_norm_mm(y, w, res, gain, *, tm, tn):
    T, K = y.shape
    N = w.shape[1]
    tm, tn = min(tm, T), min(tn, N)
    rb = min(64, tm)
    return pl.pallas_call(
        functools.partial(_res_norm_kernel, tn=tn, rb=rb),
        out_shape=jax.ShapeDtypeStruct((T, N), _F32),
        grid=(T // tm, N // tn),
        in_specs=[pl.BlockSpec((tm, K), lambda i, j: (i, 0)),
                  pl.BlockSpec((K, tn), lambda i, j: (0, j)),
                  pl.BlockSpec((tm, tn), lambda i, j: (i, j)),
                  pl.BlockSpec((1, N), lambda i, j: (0, 0))],
        out_specs=pl.BlockSpec((tm, N), lambda i, j: (i, 0)),
        compiler_params=_params("parallel", "arbitrary"),
        name="res_norm_mm",
    )(y, w, res, gain)


def _fold(gain, w):
    return (gain[:, None] * w).astype(_BF16)


def _prep_weights(mix_norm, ffn_norm, final_norm, rg_w_in, rg_conv_w, rg_conv_b, rg_w_a, rg_b_a, rg_w_i,
                  rg_b_i, rg_lambda, rg_w_out, pool_w, pool_scale, ffn_w_gate, ffn_w_up, ffn_w_down,
                  moe_router, moe_w_gate, moe_w_up, moe_w_down):
    depth, D = mix_norm.shape
    layers = []
    for i in range(depth):
        j = i // 2
        lw = {}
        if i % 2 == 0:
            lw["w_in"] = _fold(mix_norm[i], rg_w_in[j])
            lw["conv_w"] = 0.5 * rg_conv_w[j]
            lw["conv_b"] = 0.5 * rg_conv_b[j][None, :]
            lw["w_a"] = rg_w_a[j].astype(_BF16)
            lw["b_a"] = 0.5 * rg_b_a[j][:, None, :]
            lw["w_i"] = rg_w_i[j].astype(_BF16)
            lw["b_i"] = 0.5 * rg_b_i[j][:, None, :]
            lw["cneg"] = (-RG_C * jax.nn.softplus(-rg_lambda[j]))[:, None, :]
            lw["w_out"] = rg_w_out[j].astype(_BF16)
            lw["w_gate"] = _fold(ffn_norm[i], ffn_w_gate[j])
            lw["w_up"] = _fold(ffn_norm[i], ffn_w_up[j])
            lw["w_down"] = ffn_w_down[j].astype(_BF16)
        else:
            E = moe_router.shape[-1]
            G, gs = pool_w.shape[1:3]
            lw["pool_w"] = (mix_norm[i].reshape(G, gs, 1) * pool_w[j]).astype(_BF16)
            lw["pool_scale"] = pool_scale[j][None, :]
            router = jnp.pad(ffn_norm[i][:, None] * moe_router[j], ((0, 0), (0, LANES - E)))
            lw["r_hi"] = router.astype(_BF16)
            lw["r_lo"] = (router - lw["r_hi"].astype(_F32)).astype(_BF16)
            lw["n_experts"] = E
            lw["w_gate"] = _fold(ffn_norm[i], moe_w_gate[j])
            lw["w_up"] = _fold(ffn_norm[i], moe_w_up[j])
            lw["w_down"] = moe_w_down[j].reshape(-1, D).astype(_BF16)
        layers.append(lw)
    return layers, final_norm[None, :]


def _trunk(x, layers, final_gain, *, tm, tn, rg_ts, pool_ts):
    B, S, D = x.shape
    T = B * S
    xf = x.reshape(T, D)
    xb, rs = _cast_rstd(xf, tm=tm // 4)
    for i, lw in enumerate(layers):
        if i % 2 == 0:
            proj = _proj_mm(xb, rs, lw["w_in"], tm=tm, tn=2 * tn)
            y = _rg_core(proj.reshape(B, S, 2 * D), lw["conv_w"], lw["conv_b"], lw["w_a"], lw["b_a"],
                         lw["w_i"], lw["b_i"], lw["cneg"], ts=rg_ts)
            xf, xb, rs = _res_mm(y.reshape(T, D), lw["w_out"], xf, tm=tm, tn=tn)
            h = _glu_mm(xb, rs, lw["w_gate"], lw["w_up"], tm=tm, tn=tn)
        else:
            o, ob, gates, rs = _pool_mix(xf.reshape(B, S, D), lw["pool_w"], lw["pool_scale"],
                                         lw["r_hi"], lw["r_lo"], lw["n_experts"], ts=pool_ts)
            xf, xb, rs = o.reshape(T, D), ob.reshape(T, D), rs.reshape(T, LANES)
            h = _moe_glu_mm(xb, rs, lw["w_gate"], lw["w_up"], gates.reshape(T, LANES), tm=tm)
        if i + 1 < len(layers):
            xf, xb, rs = _res_mm(h, lw["w_down"], xf, tm=tm, tn=tn)
    return _res_norm_mm(h, lw["w_down"], xf, final_gain, tm=tm // 2, tn=2 * tn).reshape(B, S, D)


def kernel(x_prompt, x_sample, mix_norm, ffn_norm, final_norm, rg_w_in, rg_conv_w, rg_conv_b, rg_w_a, rg_b_a, rg_w_i, rg_b_i, rg_lambda, rg_w_out, pool_w, pool_scale, ffn_w_gate, ffn_w_up, ffn_w_down, moe_router, moe_w_gate, moe_w_up, moe_w_down):
    layers, final_gain = _prep_weights(
        mix_norm, ffn_norm, final_norm, rg_w_in, rg_conv_w, rg_conv_b, rg_w_a, rg_b_a, rg_w_i, rg_b_i,
        rg_lambda, rg_w_out, pool_w, pool_scale, ffn_w_gate, ffn_w_up, ffn_w_down, moe_router,
        moe_w_gate, moe_w_up, moe_w_down)
    y_prompt = _trunk(x_prompt, layers, final_gain, **_tiles(x_prompt.shape[1]))
    y_sample = _trunk(x_sample, layers, final_gain, **_tiles(x_sample.shape[1]))
    return (y_prompt, y_sample)
```

```python
import functools
import math

import jax
import jax.numpy as jnp
from jax import lax
from jax.experimental import pallas as pl
from jax.experimental.pallas import tpu as pltpu

NORM_EPS = 1e-6
RG_C = 8.0
RG_BLOCK = 256
CONV_WIDTH = 4
CONV_PAD_LEFT = (CONV_WIDTH - 1) // 2
POOL_WINDOWS = (2, 4, 8, 16)
TOP_K = 2
SUBLANES = 8
LANES = 128
HALO = 8
VMEM_LIMIT = 56 * 1024 * 1024
LOG2E = 1.4426950408889634
GELU_C1 = math.sqrt(2.0 / math.pi)
GELU_C3 = 0.044715 * GELU_C1
TINY = 1e-30
SCAN_LC = 64

_BF16 = jnp.bfloat16
_F32 = jnp.float32


def _tiles(seq):
    rg_ts = min(seq, 2048)
    while (2 * seq + 10 * rg_ts) * RG_BLOCK * 4 > VMEM_LIMIT - (3 << 20) and rg_ts > 8 * SCAN_LC:
        rg_ts //= 2
    return dict(tm=1024, tn=512, rg_ts=rg_ts, pool_ts=256)


def _params(*sem):
    return pltpu.CompilerParams(dimension_semantics=sem, vmem_limit_bytes=VMEM_LIMIT)


def _cast_rstd_kernel(x_ref, xb_ref, rs_ref):
    x = x_ref[...]
    xb_ref[...] = x.astype(_BF16)
    rs = lax.rsqrt(jnp.mean(x * x, axis=-1, keepdims=True) + NORM_EPS)
    rs_ref[...] = jnp.broadcast_to(rs, rs_ref.shape)


def _scale_rows(v, rs):
    return v * jnp.concatenate([rs] * (v.shape[1] // LANES), axis=1)


def _cast_rstd(x, *, tm):
    T, K = x.shape
    tm = min(tm, T)
    return pl.pallas_call(
        _cast_rstd_kernel,
        out_shape=(jax.ShapeDtypeStruct((T, K), _BF16), jax.ShapeDtypeStruct((T, LANES), _F32)),
        grid=(T // tm,),
        in_specs=[pl.BlockSpec((tm, K), lambda i: (i, 0))],
        out_specs=(pl.BlockSpec((tm, K), lambda i: (i, 0)), pl.BlockSpec((tm, LANES), lambda i: (i, 0))),
        compiler_params=_params("parallel"),
        name="cast_rstd",
    )(x)


def _sigmoid(x):
    return 1.0 / (1.0 + jnp.exp2(x * (-LOG2E)))


def _gelu_tanh(x):
    inner = x * (GELU_C1 + GELU_C3 * (x * x))
    return (0.5 * x) * (1.0 + jnp.tanh(inner))


def _proj_kernel(x_ref, rs_ref, w_ref, o_ref, *, gelu):
    o = _scale_rows(jnp.dot(x_ref[...], w_ref[...], preferred_element_type=_F32), rs_ref[...])
    o_ref[...] = (_gelu_tanh(o) if gelu else o).astype(o_ref.dtype)


def _proj_mm(xb, rs, w, *, col0, n_out, gelu, tm, tn):
    T, K = xb.shape
    tm, tn = min(tm, T), min(tn, n_out)
    c0 = col0 // tn
    return pl.pallas_call(
        functools.partial(_proj_kernel, gelu=gelu),
        out_shape=jax.ShapeDtypeStruct((T, n_out), _BF16 if gelu else _F32),
        grid=(T // tm, n_out // tn),
        in_specs=[pl.BlockSpec((tm, K), lambda i, j: (i, 0)),
                  pl.BlockSpec((tm, LANES), lambda i, j: (i, 0)),
                  pl.BlockSpec((K, tn), lambda i, j: (0, c0 + j))],
        out_specs=pl.BlockSpec((tm, tn), lambda i, j: (i, j)),
        compiler_params=_params("parallel", "parallel"),
        name="proj_gate_mm" if gelu else "proj_mm",
    )(xb, rs, w)


def _res_kernel(y_ref, w_ref, r_ref, o_ref, ob_ref, rs_ref, *, n_cols):
    j = pl.program_id(1)

    @pl.when(j == 0)
    def _():
        rs_ref[...] = jnp.zeros_like(rs_ref)

    o = r_ref[...] + jnp.dot(y_ref[...], w_ref[...], preferred_element_type=_F32)
    o_ref[...] = o
    ob_ref[...] = o.astype(_BF16)
    o2 = o * o
    part = o2[:, 0:LANES]
    for c in range(1, o.shape[1] // LANES):
        part = part + o2[:, c * LANES:(c + 1) * LANES]
    rs_ref[...] += part

    @pl.when(j == pl.num_programs(1) - 1)
    def _():
        ssq = jnp.sum(rs_ref[...], axis=-1, keepdims=True)
        rs_ref[...] = jnp.broadcast_to(lax.rsqrt(ssq * (1.0 / n_cols) + NORM_EPS), rs_ref.shape)


def _res_mm(y, w, res, *, tm, tn):
    T, K = y.shape
    N = w.shape[1]
    tm, tn = min(tm, T), min(tn, N)
    return pl.pallas_call(
        functools.partial(_res_kernel, n_cols=N),
        out_shape=(jax.ShapeDtypeStruct((T, N), _F32), jax.ShapeDtypeStruct((T, N), _BF16),
                   jax.ShapeDtypeStruct((T, LANES), _F32)),
        grid=(T // tm, N // tn),
        in_specs=[pl.BlockSpec((tm, K), lambda i, j: (i, 0)),
                  pl.BlockSpec((K, tn), lambda i, j: (0, j)),
                  pl.BlockSpec((tm, tn), lambda i, j: (i, j))],
        out_specs=(pl.BlockSpec((tm, tn), lambda i, j: (i, j)),
                   pl.BlockSpec((tm, tn), lambda i, j: (i, j)),
                   pl.BlockSpec((tm, LANES), lambda i, j: (i, 0))),
        compiler_params=_params("parallel", "arbitrary"),
        name="res_mm",
    )(y, w, res)


def _glu_kernel(x_ref, rs_ref, wg_ref, wu_ref, o_ref):
    rs = rs_ref[...]
    g = _scale_rows(jnp.dot(x_ref[...], wg_ref[...], preferred_element_type=_F32), rs)
    u = _scale_rows(jnp.dot(x_ref[...], wu_ref[...], preferred_element_type=_F32), rs)
    o_ref[...] = (g * _sigmoid(g) * u).astype(o_ref.dtype)


def _glu_mm(xb, rs, wg, wu, *, tm, tn):
    T, K = xb.shape
    N = wg.shape[1]
    tm, tn = min(tm, T), min(tn, N)
    return pl.pallas_call(
        _glu_kernel,
        out_shape=jax.ShapeDtypeStruct((T, N), _BF16),
        grid=(T // tm, N // tn),
        in_specs=[pl.BlockSpec((tm, K), lambda i, j: (i, 0)),
                  pl.BlockSpec((tm, LANES), lambda i, j: (i, 0)),
                  pl.BlockSpec((K, tn), lambda i, j: (0, j)),
                  pl.BlockSpec((K, tn), lambda i, j: (0, j))],
        out_specs=pl.BlockSpec((tm, tn), lambda i, j: (i, j)),
        compiler_params=_params("parallel", "parallel"),
        name="glu_mm",
    )(xb, rs, wg, wu)


def _moe_glu_kernel(x_ref, rs_ref, wg_ref, wu_ref, gt_ref, o_ref):
    e = pl.program_id(1)
    rs = rs_ref[...]
    g = _scale_rows(jnp.dot(x_ref[...], wg_ref[...], preferred_element_type=_F32), rs)
    u = _scale_rows(jnp.dot(x_ref[...], wu_ref[...], preferred_element_type=_F32), rs)
    gates = gt_ref[...]
    lane = lax.broadcasted_iota(jnp.int32, gates.shape, 1)
    ge = jnp.sum(jnp.where(lane == e, gates, 0.0), axis=-1, keepdims=True)
    o_ref[...] = (g * _sigmoid(g) * u * ge).astype(o_ref.dtype)


def _moe_glu_mm(xb, rs, wg, wu, gates, *, tm):
    T, K = xb.shape
    E, _, F = wg.shape
    tm = min(tm, T)
    return pl.pallas_call(
        _moe_glu_kernel,
        out_shape=jax.ShapeDtypeStruct((T, E * F), _BF16),
        grid=(T // tm, E),
        in_specs=[pl.BlockSpec((tm, K), lambda i, e: (i, 0)),
                  pl.BlockSpec((tm, LANES), lambda i, e: (i, 0)),
                  pl.BlockSpec((None, K, F), lambda i, e: (e, 0, 0)),
                  pl.BlockSpec((None, K, F), lambda i, e: (e, 0, 0)),
                  pl.BlockSpec((tm, LANES), lambda i, e: (i, 0))],
        out_specs=pl.BlockSpec((tm, F), lambda i, e: (i, e)),
        compiler_params=_params("parallel", "parallel"),
        name="moe_glu_mm",
    )(xb, rs, wg, wu, gates)


def _rg_kernel(rec_ref, prev_ref, next_ref, gate_ref, cw_ref, cb_ref, wa_ref, ba_ref, wi_ref, bi_ref,
               cn_ref, y_ref, hs_ref, us_ref, xi_ref, a_ref, b_ref, hl_ref, pl_ref, carry_ref,
               *, ts, n_chunks, lc, sb):
    p = pl.program_id(2)
    k = pl.program_id(3)
    t = jnp.where(p == 0, k, n_chunks - 1 - k)
    base = pl.multiple_of(t * ts, ts)
    C = rec_ref.shape[-1]
    G = C // LANES
    R = ts // (SUBLANES * lc)
    me = lc + CONV_WIDTH - 1
    nk = sb // SUBLANES
    per_stream = lc // nk
    row = lax.broadcasted_iota(jnp.int32, (SUBLANES, LANES), 0)
    streams = [(r, g) for r in range(R) for g in range(G)]

    @pl.when(k == 0)
    def _():
        carry_ref[...] = jnp.zeros_like(carry_ref)

    ba = ba_ref[...]
    bi = bi_ref[...]
    ch = cn_ref[...] * (0.5 * LOG2E)

    def lanes(g):
        return slice(g * LANES, (g + 1) * LANES)

    def gates(uh, r0):
        ub = uh.astype(_BF16)
        ta = jnp.tanh(jnp.dot(ub, wa_ref[...], preferred_element_type=_F32) + ba)
        ti = jnp.tanh(jnp.dot(ub, wi_ref[...], preferred_element_type=_F32) + bi)
        a = jnp.exp2(ch + ch * ta)
        x = 1.0 - a * a
        b = x * lax.rsqrt(jnp.maximum(x, TINY)) * (1.0 + ti) * uh
        for g in range(G):
            a_ref[g, pl.ds(r0, sb), :] = a[:, lanes(g)]
            b_ref[g, pl.ds(r0, sb), :] = b[:, lanes(g)]

    def piece(g, q):
        r, j = q // SUBLANES, q % SUBLANES
        return (g, pl.ds(r * (lc * SUBLANES) + j, lc, stride=SUBLANES), slice(None))

    def tile(g, r, k8):
        return (g, pl.ds(k8 + r * (lc * SUBLANES), SUBLANES), slice(None))

    def local_scan(reverse):
        def body(i, hp):
            k8 = pl.multiple_of(((lc - 1 - i) if reverse else i) * SUBLANES, SUBLANES)
            out = []
            for n, (r, g) in enumerate(streams):
                at = a_ref[tile(g, r, k8)]
                h = at * hp[2 * n] + b_ref[tile(g, r, k8)]
                pr = at * hp[2 * n + 1]
                hl_ref[tile(g, r, k8)] = h
                pl_ref[tile(g, r, k8)] = pr
                out += [h, pr]
            return tuple(out)

        init = (jnp.zeros((SUBLANES, LANES), _F32), jnp.ones((SUBLANES, LANES), _F32)) * len(streams)
        return lax.fori_loop(0, lc, body, init, unroll=4)

    def sub_chunk_states(hp, reverse):
        cvecs = {}
        edge = 0 if reverse else SUBLANES - 1
        for g in range(G):
            cur = carry_ref[g]
            for r in (range(R - 1, -1, -1) if reverse else range(R)):
                n = streams.index((r, g))
                b, a = hp[2 * n], hp[2 * n + 1]
                for d in (1, 2, 4):
                    shift, valid = (SUBLANES - d, row < SUBLANES - d) if reverse else (d, row >= d)
                    b = b + a * jnp.where(valid, pltpu.roll(b, shift, 0), 0.0)
                    a = a * jnp.where(valid, pltpu.roll(a, shift, 0), 1.0)
                after = b + a * cur
                first = row == (SUBLANES - 1 if reverse else 0)
                cvecs[(r, g)] = jnp.where(first, cur, pltpu.roll(after, SUBLANES - 1 if reverse else 1, 0))
                cur = jnp.broadcast_to(after[edge:edge + 1, :], (SUBLANES, LANES))
            carry_ref[g] = cur
        return cvecs

    def sub_block(s):
        r, h = s // per_stream, s % per_stream
        return (pl.multiple_of((r * lc + h * nk) * SUBLANES, SUBLANES),
                pl.multiple_of((r * me + h * nk) * SUBLANES, SUBLANES))

    @pl.when(p == 0)
    def _():
        for q in range(SUBLANES * R):
            r, j = divmod(q, SUBLANES)
            for g in range(G):
                xi_ref[g, pl.ds((r * me + 1) * SUBLANES + j, lc, stride=SUBLANES), :] = (
                    rec_ref[pl.ds(q * lc, lc), lanes(g)])

        def step_tile(g, r, kk):
            return xi_ref[g, pl.ds((r * me + 1 + kk) * SUBLANES, SUBLANES), :]

        def bcast(v):
            return jnp.broadcast_to(v, (SUBLANES, LANES))

        for g in range(G):
            before = jnp.where(t > 0, prev_ref[HALO - 1:HALO, lanes(g)], 0.0)
            after = jnp.where(t < n_chunks - 1, next_ref[0:2, lanes(g)], 0.0)
            for r in range(R):
                fill = step_tile(g, r - 1, lc - 1)[SUBLANES - 1:SUBLANES, :] if r > 0 else before
                xi_ref[g, pl.ds(r * me * SUBLANES, SUBLANES), :] = jnp.where(
                    row == 0, bcast(fill), pltpu.roll(step_tile(g, r, lc - 1), 1, 0))
                for d in range(2):
                    fill = step_tile(g, r + 1, d)[0:1, :] if r + 1 < R else after[d:d + 1, :]
                    xi_ref[g, pl.ds((r * me + 1 + lc + d) * SUBLANES, SUBLANES), :] = jnp.where(
                        row == SUBLANES - 1, bcast(fill), pltpu.roll(step_tile(g, r, d), SUBLANES - 1, 0))

        cw = cw_ref[...]
        cb = cb_ref[...]

        def gates_body(s, _):
            r0, x0 = sub_block(s)
            parts = []
            for g in range(G):
                win = xi_ref[g, pl.ds(x0, sb + (CONV_WIDTH - 1) * SUBLANES), :]
                uh = cb[:, lanes(g)]
                for tap in range(CONV_WIDTH):
                    uh = uh + cw[tap:tap + 1, lanes(g)] * win[tap * SUBLANES:tap * SUBLANES + sb, :]
                us_ref[g, pl.ds(base + r0, sb), :] = uh
                parts.append(uh)
            gates(jnp.concatenate(parts, axis=1), r0)
            return 0

        lax.fori_loop(0, ts // sb, gates_body, 0, unroll=2)
        cvecs = sub_chunk_states(local_scan(False), False)

        def fix_body(kk, _):
            k8 = pl.multiple_of(kk * SUBLANES, SUBLANES)
            for r, g in streams:
                hs_ref[tile(g, r, base + k8)] = hl_ref[tile(g, r, k8)] + pl_ref[tile(g, r, k8)] * cvecs[(r, g)]
            return 0

        lax.fori_loop(0, lc, fix_body, 0, unroll=4)

    @pl.when(p == 1)
    def _():
        def gates_body(s, _):
            r0, _ = sub_block(s)
            gates(jnp.concatenate([us_ref[g, pl.ds(base + r0, sb), :] for g in range(G)], axis=1), r0)
            return 0

        lax.fori_loop(0, ts // sb, gates_body, 0, unroll=4)
        cvecs = sub_chunk_states(local_scan(True), True)

        def fix_body(kk, _):
            k8 = pl.multiple_of(kk * SUBLANES, SUBLANES)
            for r, g in streams:
                h = hl_ref[tile(g, r, k8)] + pl_ref[tile(g, r, k8)] * cvecs[(r, g)]
                a_ref[tile(g, r, k8)] = h + hs_ref[tile(g, r, base + k8)]
            return 0

        lax.fori_loop(0, lc, fix_body, 0, unroll=4)

        def out_body(s, _):
            r0 = pl.multiple_of(s * sb, sb)
            gate = gate_ref[pl.ds(r0, sb), :].astype(_F32)
            for i in range(sb // lc):
                for g in range(G):
                    y_ref[pl.ds(r0 + i * lc, lc), lanes(g)] = (
                        a_ref[piece(g, s * (sb // lc) + i)] * gate[i * lc:(i + 1) * lc, lanes(g)]
                    ).astype(y_ref.dtype)
            return 0

        lax.fori_loop(0, ts // sb, out_body, 0)


def _rg_core(gate, rec, conv_w, conv_b, w_a, b_a, w_i, b_i, cneg, *, ts, lc=SCAN_LC):
    B, S, D = rec.shape
    C = RG_BLOCK
    G = C // LANES
    nc = D // C
    ts = min(ts, S)
    n = S // ts
    lc = min(lc, ts // SUBLANES)
    sb = min(256, lc * SUBLANES)
    assert lc % SUBLANES == 0 and ts % (SUBLANES * lc) == 0 and (lc * SUBLANES) % sb == 0 and sb % lc == 0
    assert (CONV_WIDTH, CONV_PAD_LEFT) == (4, 1)
    tpb = ts // HALO

    def t_of(p, k):
        return jnp.where(p == 0, k, n - 1)

    def t_out(p, k):
        return jnp.where(p == 0, n - 1, n - 1 - k)

    kern = functools.partial(_rg_kernel, ts=ts, n_chunks=n, lc=lc, sb=sb)
    return pl.pallas_call(
        kern,
        out_shape=jax.ShapeDtypeStruct((B, S, D), _BF16),
        grid=(B, nc, 2, n),
        in_specs=[
            pl.BlockSpec((None, ts, C), lambda b, c, p, k: (b, t_of(p, k), c)),
            pl.BlockSpec((None, HALO, C), lambda b, c, p, k: (b, jnp.maximum(t_of(p, k) * tpb - 1, 0), c)),
            pl.BlockSpec((None, HALO, C),
                         lambda b, c, p, k: (b, jnp.minimum((t_of(p, k) + 1) * tpb, S // HALO - 1), c)),
            pl.BlockSpec((None, ts, C), lambda b, c, p, k: (b, t_out(p, k), c)),
            pl.BlockSpec((CONV_WIDTH, C), lambda b, c, p, k: (0, c)),
            pl.BlockSpec((1, C), lambda b, c, p, k: (0, c)),
            pl.BlockSpec((None, None, C, C), lambda b, c, p, k: (p, c, 0, 0)),
            pl.BlockSpec((None, 1, C), lambda b, c, p, k: (p, 0, c)),
            pl.BlockSpec((None, None, C, C), lambda b, c, p, k: (p, c, 0, 0)),
            pl.BlockSpec((None, 1, C), lambda b, c, p, k: (p, 0, c)),
            pl.BlockSpec((None, 1, C), lambda b, c, p, k: (p, 0, c)),
        ],
        out_specs=pl.BlockSpec((None, ts, C), lambda b, c, p, k: (b, t_out(p, k), c)),
        scratch_shapes=[
            pltpu.VMEM((G, S, LANES), _F32),
            pltpu.VMEM((G, S, LANES), _F32),
            pltpu.VMEM((G, (ts // lc // SUBLANES) * (lc + CONV_WIDTH - 1) * SUBLANES, LANES), _F32),
            pltpu.VMEM((G, ts, LANES), _F32),
            pltpu.VMEM((G, ts, LANES), _F32),
            pltpu.VMEM((G, ts, LANES), _F32),
            pltpu.VMEM((G, ts, LANES), _F32),
            pltpu.VMEM((G, SUBLANES, LANES), _F32),
        ],
        compiler_params=_params("parallel", "parallel", "arbitrary", "arbitrary"),
        name="rg_core",
    )(rec, rec, rec, gate, conv_w, conv_b, w_a, b_a, w_i, b_i, cneg)


def _pool_kernel(x_ref, prev_ref, next_ref, pw_ref, ps_ref, rh_ref, rl_ref,
                 o_ref, ob_ref, gt_ref, rs_ref, ext_ref, *, ts, n_chunks, seq, n_experts):
    t = pl.program_id(1)
    D = x_ref.shape[-1]
    G = len(POOL_WINDOWS)
    gs = D // G
    L = ts + 2 * HALO
    assert POOL_WINDOWS[-1] == 2 * HALO and all(w <= HALO for w in POOL_WINDOWS[:-1])

    def scaled(v):
        return v * lax.rsqrt(jnp.mean(v * v, axis=-1, keepdims=True) + NORM_EPS)

    ext_ref[pl.ds(HALO, ts), :] = scaled(x_ref[...])
    ext_ref[pl.ds(0, HALO), :] = jnp.where(t > 0, scaled(prev_ref[...]), 0.0)
    ext_ref[pl.ds(HALO + ts, HALO), :] = jnp.where(t < n_chunks - 1, scaled(next_ref[...]), 0.0)

    def ahead(v, n):
        return pltpu.roll(v, L - n, 0)

    tpos = t * ts + lax.broadcasted_iota(jnp.int32, (ts, 1), 0)
    ssq = jnp.zeros((ts, 1), _F32)
    logits = jnp.zeros((ts, LANES), _F32)
    for gi, win in enumerate(POOL_WINDOWS):
        cols = pl.ds(gi * gs, gs)
        e = ext_ref[:, cols]
        if win == 2:
            s = (pltpu.roll(e, 1, 0) + e)[HALO:HALO + ts, :]
        else:
            f, w = e, 1
            while w < min(win, HALO):
                f = f + ahead(f, w)
                w *= 2
            if win < 2 * HALO:
                s = pltpu.roll(f, win // 2, 0)[HALO:HALO + ts, :]
            else:
                s = f[0:ts, :] + f[HALO:HALO + ts, :]
        half = win // 2
        cnt = jnp.minimum(tpos + (win - half), seq) - jnp.maximum(tpos - half, 0)
        d = s * (1.0 / cnt.astype(_F32)) - e[HALO:HALO + ts, :]
        y = jnp.dot(d.astype(_BF16), pw_ref[gi], preferred_element_type=_F32)
        o = x_ref[:, cols] + y * ps_ref[:, cols]
        o_ref[:, cols] = o
        oh = o.astype(_BF16)
        ob_ref[:, cols] = oh
        ssq = ssq + jnp.sum(o * o, axis=-1, keepdims=True)
        ol = (o - oh.astype(_F32)).astype(_BF16)
        rh = rh_ref[cols, :]
        logits = (logits + jnp.dot(oh, rh, preferred_element_type=_F32)
                  + jnp.dot(oh, rl_ref[cols, :], preferred_element_type=_F32)
                  + jnp.dot(ol, rh, preferred_element_type=_F32))

    rs = lax.rsqrt(ssq * (1.0 / D) + NORM_EPS)
    rs_ref[...] = jnp.broadcast_to(rs, rs_ref.shape)
    logits = logits * rs
    lane = lax.broadcasted_iota(jnp.int32, (ts, LANES), 1).astype(_F32)
    neg = jnp.float32(-jnp.inf)
    logits = jnp.where(lane < n_experts, logits, neg)
    m1 = jnp.max(logits, axis=-1, keepdims=True)
    i1 = jnp.min(jnp.where(logits == m1, lane, float(LANES)), axis=-1, keepdims=True)
    rest = jnp.where(lane == i1, neg, logits)
    m2 = jnp.max(rest, axis=-1, keepdims=True)
    i2 = jnp.min(jnp.where(rest == m2, lane, float(LANES)), axis=-1, keepdims=True)
    e2 = jnp.exp(m2 - m1)
    g1 = 1.0 / (1.0 + e2)
    gt_ref[...] = jnp.where(lane == i1, g1, 0.0) + jnp.where(lane == i2, e2 * g1, 0.0)


def _pool_mix(x, pool_w, pool_scale, r_hi, r_lo, n_experts, *, ts):
    B, S, D = x.shape
    ts = min(ts, S)
    n = S // ts
    tpb = ts // HALO
    G, gs, _ = pool_w.shape
    kern = functools.partial(_pool_kernel, ts=ts, n_chunks=n, seq=S, n_experts=n_experts)
    return pl.pallas_call(
        kern,
        out_shape=(jax.ShapeDtypeStruct((B, S, D), _F32), jax.ShapeDtypeStruct((B, S, D), _BF16),
                   jax.ShapeDtypeStruct((B, S, LANES), _F32), jax.ShapeDtypeStruct((B, S, LANES), _F32)),
        grid=(B, n),
        in_specs=[
            pl.BlockSpec((None, ts, D), lambda b, t: (b, t, 0)),
            pl.BlockSpec((None, HALO, D), lambda b, t: (b, jnp.maximum(t * tpb - 1, 0), 0)),
            pl.BlockSpec((None, HALO, D), lambda b, t: (b, jnp.minimum((t + 1) * tpb, S // HALO - 1), 0)),
            pl.BlockSpec((G, gs, gs), lambda b, t: (0, 0, 0)),
            pl.BlockSpec((1, D), lambda b, t: (0, 0)),
            pl.BlockSpec((D, LANES), lambda b, t: (0, 0)),
            pl.BlockSpec((D, LANES), lambda b, t: (0, 0)),
        ],
        out_specs=(pl.BlockSpec((None, ts, D), lambda b, t: (b, t, 0)),
                   pl.BlockSpec((None, ts, D), lambda b, t: (b, t, 0)),
                   pl.BlockSpec((None, ts, LANES), lambda b, t: (b, t, 0)),
                   pl.BlockSpec((None, ts, LANES), lambda b, t: (b, t, 0))),
        scratch_shapes=[pltpu.VMEM((ts + 2 * HALO, D), _F32)],
        compiler_params=_params("parallel", "parallel"),
        name="pool_mix",
    )(x, x, x, pool_w, pool_scale, r_hi, r_lo)


def _res_norm_kernel(y_ref, w_ref, r_ref, g_ref, o_ref, *, tn, rb):
    j = pl.program_id(1)
    c0 = pl.multiple_of(j * tn, tn)
    o_ref[:, pl.ds(c0, tn)] = r_ref[...] + jnp.dot(y_ref[...], w_ref[...], preferred_element_type=_F32)

    @pl.when(j == pl.num_programs(1) - 1)
    def _():
        def body(s, _):
            r0 = pl.multiple_of(s * rb, rb)
            x = o_ref[pl.ds(r0, rb), :]
            o_ref[pl.ds(r0, rb), :] = (x * lax.rsqrt(jnp.mean(x * x, axis=-1, keepdims=True) + NORM_EPS)
                                       * g_ref[...])
            return 0

        lax.fori_loop(0, o_ref.shape[0] // rb, body, 0)


def _res_norm_mm(y, w, res, gain, *, tm, tn):
    T, K = y.shape
    N = w.shape[1]
    tm, tn = min(tm, T), min(tn, N)
    rb = min(64, tm)
    return pl.pallas_call(
        functools.partial(_res_norm_kernel, tn=tn, rb=rb),
        out_shape=jax.ShapeDtypeStruct((T, N), _F32),
        grid=(T // tm, N // tn),
        in_specs=[pl.BlockSpec((tm, K), lambda i, j: (i, 0)),
                  pl.BlockSpec((K, tn), lambda i, j: (0, j)),
                  pl.BlockSpec((tm, tn), lambda i, j: (i, j)),
                  pl.BlockSpec((1, N), lambda i, j: (0, 0))],
        out_specs=pl.BlockSpec((tm, N), lambda i, j: (i, 0)),
        compiler_params=_params("parallel", "arbitrary"),
        name="res_norm_mm",
    )(y, w, res, gain)


def _fold(gain, w):
    return (gain[:, None] * w).astype(_BF16)


def _prep_weights(mix_norm, ffn_norm, final_norm, rg_w_in, rg_conv_w, rg_conv_b, rg_w_a, rg_b_a, rg_w_i,
                  rg_b_i, rg_lambda, rg_w_out, pool_w, pool_scale, ffn_w_gate, ffn_w_up, ffn_w_down,
                  moe_router, moe_w_gate, moe_w_up, moe_w_down):
    depth, D = mix_norm.shape
    layers = []
    for i in range(depth):
        j = i // 2
        lw = {}
        if i % 2 == 0:
            lw["w_in"] = _fold(mix_norm[i], rg_w_in[j])
            lw["conv_w"] = 0.5 * rg_conv_w[j]
            lw["conv_b"] = 0.5 * rg_conv_b[j][None, :]
            lw["w_a"] = rg_w_a[j].astype(_BF16)
            lw["b_a"] = 0.5 * rg_b_a[j][:, None, :]
            lw["w_i"] = rg_w_i[j].astype(_BF16)
            lw["b_i"] = 0.5 * rg_b_i[j][:, None, :]
            lw["cneg"] = (-RG_C * jax.nn.softplus(-rg_lambda[j]))[:, None, :]
            lw["w_out"] = rg_w_out[j].astype(_BF16)
            lw["w_gate"] = _fold(ffn_norm[i], ffn_w_gate[j])
            lw["w_up"] = _fold(ffn_norm[i], ffn_w_up[j])
            lw["w_down"] = ffn_w_down[j].astype(_BF16)
        else:
            E = moe_router.shape[-1]
            G, gs = pool_w.shape[1:3]
            lw["pool_w"] = (mix_norm[i].reshape(G, gs, 1) * pool_w[j]).astype(_BF16)
            lw["pool_scale"] = pool_scale[j][None, :]
            router = jnp.pad(ffn_norm[i][:, None] * moe_router[j], ((0, 0), (0, LANES - E)))
            lw["r_hi"] = router.astype(_BF16)
            lw["r_lo"] = (router - lw["r_hi"].astype(_F32)).astype(_BF16)
            lw["n_experts"] = E
            lw["w_gate"] = _fold(ffn_norm[i], moe_w_gate[j])
            lw["w_up"] = _fold(ffn_norm[i], moe_w_up[j])
            lw["w_down"] = moe_w_down[j].reshape(-1, D).astype(_BF16)
        layers.append(lw)
    return layers, final_norm[None, :]


def _trunk(x, layers, final_gain, *, tm, tn, rg_ts, pool_ts):
    B, S, D = x.shape
    T = B * S
    xf = x.reshape(T, D)
    xb, rs = _cast_rstd(xf, tm=tm // 4)
    for i, lw in enumerate(layers):
        if i % 2 == 0:
            gate = _proj_mm(xb, rs, lw["w_in"], col0=0, n_out=D, gelu=True, tm=tm, tn=2 * tn)
            rec = _proj_mm(xb, rs, lw["w_in"], col0=D, n_out=D, gelu=False, tm=tm, tn=2 * tn)
            y = _rg_core(gate.reshape(B, S, D), rec.reshape(B, S, D), lw["conv_w"], lw["conv_b"], lw["w_a"], lw["b_a"],
                         lw["w_i"], lw["b_i"], lw["cneg"], ts=rg_ts)
            xf, xb, rs = _res_mm(y.reshape(T, D), lw["w_out"], xf, tm=tm, tn=tn)
            h = _glu_mm(xb, rs, lw["w_gate"], lw["w_up"], tm=tm, tn=tn)
        else:
            o, ob, gates, rs = _pool_mix(xf.reshape(B, S, D), lw["pool_w"], lw["pool_scale"],
                                         lw["r_hi"], lw["r_lo"], lw["n_experts"], ts=pool_ts)
            xf, xb, rs = o.reshape(T, D), ob.reshape(T, D), rs.reshape(T, LANES)
            h = _moe_glu_mm(xb, rs, lw["w_gate"], lw["w_up"], gates.reshape(T, LANES), tm=tm)
        if i + 1 < len(layers):
            xf, xb, rs = _res_mm(h, lw["w_down"], xf, tm=tm, tn=tn)
    return _res_norm_mm(h, lw["w_down"], xf, final_gain, tm=tm // 2, tn=2 * tn).reshape(B, S, D)


def kernel(x_prompt, x_sample, mix_norm, ffn_norm, final_norm, rg_w_in, rg_conv_w, rg_conv_b, rg_w_a, rg_b_a, rg_w_i, rg_b_i, rg_lambda, rg_w_out, pool_w, pool_scale, ffn_w_gate, ffn_w_up, ffn_w_down, moe_router, moe_w_gate, moe_w_up, moe_w_down):
    layers, final_gain = _prep_weights(
        mix_norm, ffn_norm, final_norm, rg_w_in, rg_conv_w, rg_conv_b, rg_w_a, rg_b_a, rg_w_i, rg_b_i,
        rg_lambda, rg_w_out, pool_w, pool_scale, ffn_w_gate, ffn_w_up, ffn_w_down, moe_router,
        moe_w_gate, moe_w_up, moe_w_down)
    y_prompt = _trunk(x_prompt, layers, final_gain, **_tiles(x_prompt.shape[1]))
    y_sample = _trunk(x_sample, layers, final_gain, **_tiles(x_sample.shape[1]))
    return (y_prompt, y_sample)
```
